```python
import math
import jax
import jax.numpy as jnp
from jax import lax
import numpy as np

D_MODEL = 1024
BATCH = 2
SEQ = 16384
DEPTH = 4

GRID_W = 64
CTX_LEN = 256
N_MIXERS = 3
N_ATTN_LAYERS = (DEPTH + 2) // 3
N_SSD_LAYERS = (DEPTH + 1) // 3
N_POOL_LAYERS = DEPTH // 3

N_HEADS = 16
N_KV_HEADS = 4
GQA_GROUP = N_HEADS // N_KV_HEADS
HEAD_DIM = D_MODEL // N_HEADS
ROPE_HALF = HEAD_DIM // 2
ROPE_FREQS = HEAD_DIM // 4
ROPE_THETA = 10000.0
Q_BLOCK = 128

SSM_EXPAND = 2
D_INNER = SSM_EXPAND * D_MODEL
SSM_HEADDIM = 64
SSM_HEADS = D_INNER // SSM_HEADDIM
SSM_GROUPS = 4
HEADS_PER_GROUP = SSM_HEADS // SSM_GROUPS
SSM_STATE = 128
CONV_WIDTH = 5
CONV_DIM = D_INNER + 2 * SSM_GROUPS * SSM_STATE
IN_PROJ_DIM = D_INNER + CONV_DIM + 2 * SSM_HEADS
SSD_CHUNK = 128

POOL_WINDOWS = (2, 4, 8, 16)
N_POOL_GROUPS = len(POOL_WINDOWS)
POOL_GROUP_DIM = D_MODEL // N_POOL_GROUPS

N_EXPERTS = 16
D_EXPERT = 2048
EC_CAPACITY_FACTOR = 2

DEEPNORM_ALPHA = (2.0 * DEPTH) ** 0.25
DEEPNORM_BETA = (8.0 * DEPTH) ** -0.25
LN_EPS = 1e-5
RMS_EPS = 1e-6

kernel_name = "hybrid_dit_attn_ssd_pool_ecmoe"


def layer_norm(x, g, b):
    xf = x.astype(jnp.float32)
    mu = jnp.mean(xf, axis=-1, keepdims=True)
    var = jnp.mean(jnp.square(xf - mu), axis=-1, keepdims=True)
    return ((xf - mu) * lax.rsqrt(var + LN_EPS) * g + b).astype(x.dtype)


def rms_norm(x, g):
    xf = x.astype(jnp.float32)
    ms = jnp.mean(jnp.square(xf), axis=-1, keepdims=True)
    return (xf * lax.rsqrt(ms + RMS_EPS) * g).astype(x.dtype)


def modulation(cond, w_mod, b_mod):
    m = (jax.nn.silu(cond) @ w_mod + b_mod)[..., None, :]
    return jnp.split(m, 6, axis=-1)


def modulate(h, shift, scale):
    return h * (1 + scale) + shift


def axial_rope_tables(n_tokens):
    rows = n_tokens // GRID_W
    row_idx = jnp.repeat(jnp.arange(rows, dtype=jnp.int32), GRID_W).astype(jnp.float32)
    col_idx = jnp.tile(jnp.arange(GRID_W, dtype=jnp.int32), rows).astype(jnp.float32)
    inv_freq = ROPE_THETA ** (-jnp.arange(ROPE_FREQS, dtype=jnp.float32) / ROPE_FREQS)
    ang = jnp.concatenate([row_idx[:, None] * inv_freq, col_idx[:, None] * inv_freq], axis=-1)
    return jnp.cos(ang), jnp.sin(ang)


def apply_rope(t, cos, sin):
    tf = t.astype(jnp.float32)
    t1, t2 = tf[..., :ROPE_HALF], tf[..., ROPE_HALF:]
    c, s = cos[None, :, None, :], sin[None, :, None, :]
    return jnp.concatenate([t1 * c - t2 * s, t2 * c + t1 * s], axis=-1).astype(t.dtype)


def project_qkv(h, w_qkv, q_g, k_g):
    b, n = h.shape[:2]
    q, k, v = jnp.split(h @ w_qkv, [N_HEADS * HEAD_DIM, (N_HEADS + N_KV_HEADS) * HEAD_DIM], axis=-1)
    q = rms_norm(q.reshape(b, n, N_HEADS, HEAD_DIM), q_g)
    k = rms_norm(k.reshape(b, n, N_KV_HEADS, HEAD_DIM), k_g)
    return q, k, v.reshape(b, n, N_KV_HEADS, HEAD_DIM)


def gqa_attend(q, k, v):
    b, nq = q.shape[:2]
    qg = q.reshape(b, nq, N_KV_HEADS, GQA_GROUP, HEAD_DIM)
    s = jnp.einsum('bqkgd,bskd->bkgqs', qg, k).astype(jnp.float32) * (HEAD_DIM ** -0.5)
    p = jax.nn.softmax(s, axis=-1)
    o = jnp.einsum('bkgqs,bskd->bqkgd', p.astype(v.dtype), v)
    return o.reshape(b, nq, N_HEADS * HEAD_DIM)


def attention_mixer(h_lat, h_ctx, w_qkv, w_o, q_g, k_g, cos, sin, need_ctx_out):
    b, n = h_lat.shape[:2]
    q_l, k_l, v_l = project_qkv(h_lat, w_qkv, q_g, k_g)
    q_l, k_l = apply_rope(q_l, cos, sin), apply_rope(k_l, cos, sin)
    q_c, k_c, v_c = project_qkv(h_ctx, w_qkv, q_g, k_g)
    k_all = jnp.concatenate([k_c, k_l], axis=1)
    v_all = jnp.concatenate([v_c, v_l], axis=1)
    q_blocks = q_l.reshape(b, n // Q_BLOCK, Q_BLOCK, N_HEADS, HEAD_DIM).swapaxes(0, 1)
    o_blocks = lax.map(lambda qb: gqa_attend(qb, k_all, v_all), q_blocks)
    y_lat = o_blocks.swapaxes(0, 1).reshape(b, n, N_HEADS * HEAD_DIM) @ w_o
    y_ctx = gqa_attend(q_c, k_c, v_c) @ w_o if need_ctx_out else None
    return y_lat, y_ctx


def centred_depthwise_conv(x, w, bias):
    pad = CONV_WIDTH // 2
    y = lax.conv_general_dilated(x, w[:, None, :].astype(x.dtype), window_strides=(1,),
                                 padding=[(pad, pad)], dimension_numbers=('NWC', 'WIO', 'NWC'),
                                 feature_group_count=x.shape[-1])
    return y + bias


def ssd_scan(xs, dt, a, bm, cm, h0):
    b, L = xs.shape[:2]
    nc = L // SSD_CHUNK
    rs = lambda t: t.reshape(b, nc, SSD_CHUNK, *t.shape[2:])
    xs, dt, bm, cm = rs(xs), rs(dt), rs(bm), rs(cm)
    acs = jnp.cumsum(dt * a, axis=2)
    tri = jnp.tril(jnp.ones((SSD_CHUNK, SSD_CHUNK), dtype=bool))[:, :, None, None]
    diff = acs[:, :, :, None] - acs[:, :, None, :]
    decay = jnp.exp(jnp.where(tri, diff, -jnp.inf))
    cb = jnp.einsum('bclgn,bcsgn->bclsg', cm, bm)
    mix = cb[..., None] * decay * dt[:, :, None]
    y_diag = jnp.einsum('bclsgh,bcsghp->bclghp', mix, xs)
    decay_to_end = jnp.exp(acs[:, :, -1:] - acs)
    states = jnp.einsum('bcsgn,bcsgh,bcsghp->bcghpn', bm, decay_to_end * dt, xs)
    chunk_decay = jnp.exp(acs[:, :, -1])

    def step(h, inp):
        st, dec = inp
        return h * dec[..., None, None] + st, h

    h_final, h_prev = lax.scan(step, h0, (jnp.moveaxis(states, 1, 0), jnp.moveaxis(chunk_decay, 1, 0)))
    h_prev = jnp.moveaxis(h_prev, 0, 1)
    y_off = jnp.einsum('bclgn,bcghpn,bclgh->bclghp', cm, h_prev, jnp.exp(acs))
    return (y_diag + y_off).reshape(b, L, SSM_GROUPS, HEADS_PER_GROUP, SSM_HEADDIM), h_final


def ssd_mixer(h_lat, h_ctx, w_in, conv_w, conv_b, dt_bias, a_log, d_skip, norm_g, w_out, need_ctx_out):
    a = -jnp.exp(a_log.astype(jnp.float32)).reshape(2, SSM_GROUPS, HEADS_PER_GROUP)
    dtb = dt_bias.astype(jnp.float32).reshape(2, SSM_GROUPS, HEADS_PER_GROUP)
    dsk = d_skip.reshape(SSM_GROUPS, HEADS_PER_GROUP)[:, :, None]

    def project(h):
        b, n = h.shape[:2]
        z, xbc, dt = jnp.split(h @ w_in, [D_INNER, D_INNER + CONV_DIM], axis=-1)
        xbc = jax.nn.silu(centred_depthwise_conv(xbc, conv_w, conv_b))
        xs, bm, cm = jnp.split(xbc, [D_INNER, D_INNER + SSM_GROUPS * SSM_STATE], axis=-1)
        dt = jax.nn.softplus(dt.astype(jnp.float32).reshape(b, n, 2, SSM_GROUPS, HEADS_PER_GROUP) + dtb)
        return (z, xs.reshape(b, n, SSM_GROUPS, HEADS_PER_GROUP, SSM_HEADDIM),
                bm.reshape(b, n, SSM_GROUPS, SSM_STATE), cm.reshape(b, n, SSM_GROUPS, SSM_STATE), dt)

    def bidirectional(seq, h0_f, h0_b):
        z, xs, bm, cm, dt = seq
        b, n = xs.shape[:2]
        y_f, hf = ssd_scan(xs, dt[:, :, 0], a[0], bm, cm, h0_f)
        y_b, hb = ssd_scan(xs[:, ::-1], dt[:, ::-1, 1], a[1], bm[:, ::-1], cm[:, ::-1], h0_b)
        y = y_f + y_b[:, ::-1] + dsk * xs
        y = y.reshape(b, n, D_INNER).astype(z.dtype) * jax.nn.silu(z)
        return rms_norm(y, norm_g) @ w_out, hf, hb

    b = h_lat.shape[0]
    h0 = jnp.zeros((b, SSM_GROUPS, HEADS_PER_GROUP, SSM_HEADDIM, SSM_STATE), jnp.float32)
    y_ctx, hc_f, hc_b = bidirectional(project(h_ctx), h0, h0)
    y_lat, _, _ = bidirectional(project(h_lat), hc_f, hc_b)
    return y_lat, (y_ctx if need_ctx_out else None)


def multiscale_pool_mixer(h, w_pool, pool_scale):
    b, n, d = h.shape
    hf = h.astype(jnp.float32)
    csum = jnp.concatenate([jnp.zeros((b, 1, d), jnp.float32), jnp.cumsum(hf, axis=1)], axis=1)
    t = jnp.arange(n)
    groups = []
    for gi, win in enumerate(POOL_WINDOWS):
        lo = jnp.clip(t - win // 2, 0, n)
        hi = jnp.clip(t + win - win // 2, 0, n)
        sl = slice(gi * POOL_GROUP_DIM, (gi + 1) * POOL_GROUP_DIM)
        cg = csum[:, :, sl]
        mean = (cg[:, hi] - cg[:, lo]) / (hi - lo).astype(jnp.float32)[None, :, None]
        groups.append(mean - hf[:, :, sl])
    pooled = jnp.stack(groups, axis=2).astype(h.dtype)
    y = jnp.einsum('bngc,gce->bnge', pooled, w_pool).reshape(b, n, d)
    return y * pool_scale


def expert_choice_moe(h, w_router, w_gate, w_up, w_down):
    b, n, d = h.shape
    cap = EC_CAPACITY_FACTOR * n // N_EXPERTS
    aff = jax.nn.softmax((h @ w_router).astype(jnp.float32), axis=-1)
    gate, idx = lax.top_k(jnp.swapaxes(aff, 1, 2), cap)
    xe = jax.vmap(lambda hb, ib: hb[ib])(h, idx)
    hid = jax.nn.silu(jnp.einsum('becd,edf->becf', xe, w_gate)) * jnp.einsum('becd,edf->becf', xe, w_up)
    ye = jnp.einsum('becf,efd->becd', hid, w_down) * gate[..., None].astype(h.dtype)
    return jax.vmap(lambda yb, ib: jnp.zeros((n, d), yb.dtype).at[ib.reshape(-1)].add(yb.reshape(-1, d)))(ye, idx)


def setup_inputs(seed: int = 0) -> dict:
    key = jax.random.key(seed)
    ks = jax.random.split(key, 32)
    f32 = jnp.float32
    d = D_MODEL

    def nrm(k, shape, s):
        return jax.random.normal(k, shape, f32) * s

    dt0 = jnp.exp(jax.random.uniform(ks[17], (N_SSD_LAYERS, 2, SSM_HEADS), f32, math.log(1e-3), math.log(1e-1)))
    return {
        'x': nrm(ks[0], (BATCH, SEQ, d), 1.0),
        'c': nrm(ks[1], (BATCH, d), 1.0),
        'ctx': nrm(ks[2], (BATCH, CTX_LEN, d), 1.0),
        'c_ctx': nrm(ks[3], (d,), 1.0),
        'w_mod': nrm(ks[4], (DEPTH, d, 6 * d), 0.5 * d ** -0.5),
        'b_mod': nrm(ks[5], (DEPTH, 6 * d), 0.01),
        'ln1_g': 1.0 + nrm(ks[6], (DEPTH, d), 0.05),
        'ln1_b': nrm(ks[7], (DEPTH, d), 0.01),
        'ln2_g': 1.0 + nrm(ks[8], (DEPTH, d), 0.05),
        'ln2_b': nrm(ks[9], (DEPTH, d), 0.01),
        'attn_w_qkv': nrm(ks[10], (N_ATTN_LAYERS, d, (N_HEADS + 2 * N_KV_HEADS) * HEAD_DIM), d ** -0.5),
        'attn_w_o': nrm(ks[11], (N_ATTN_LAYERS, N_HEADS * HEAD_DIM, d), DEEPNORM_BETA * (N_HEADS * HEAD_DIM) ** -0.5),
        'attn_q_g': 1.0 + nrm(ks[12], (N_ATTN_LAYERS, HEAD_DIM), 0.05),
        'attn_k_g': 1.0 + nrm(ks[13], (N_ATTN_LAYERS, HEAD_DIM), 0.05),
        'ssd_w_in': nrm(ks[14], (N_SSD_LAYERS, d, IN_PROJ_DIM), d ** -0.5),
        'ssd_conv_w': nrm(ks[15], (N_SSD_LAYERS, CONV_WIDTH, CONV_DIM), CONV_WIDTH ** -0.5),
        'ssd_conv_b': nrm(ks[16], (N_SSD_LAYERS, CONV_DIM), 0.01),
        'ssd_dt_bias': dt0 + jnp.log(-jnp.expm1(-dt0)),
        'ssd_a_log': jnp.log(jax.random.uniform(ks[18], (N_SSD_LAYERS, 2, SSM_HEADS), f32, 1.0, 16.0)),
        'ssd_d': 1.0 + nrm(ks[19], (N_SSD_LAYERS, SSM_HEADS), 0.1),
        'ssd_norm_g': 1.0 + nrm(ks[20], (N_SSD_LAYERS, D_INNER), 0.05),
        'ssd_w_out': nrm(ks[21], (N_SSD_LAYERS, D_INNER, d), DEEPNORM_BETA * D_INNER ** -0.5),
        'pool_w': nrm(ks[22], (N_POOL_LAYERS, N_POOL_GROUPS, POOL_GROUP_DIM, POOL_GROUP_DIM), DEEPNORM_BETA * POOL_GROUP_DIM ** -0.5),
        'pool_scale': 1.0 + nrm(ks[23], (N_POOL_LAYERS, d), 0.1),
        'moe_router': nrm(ks[24], (DEPTH, d, N_EXPERTS), d ** -0.5),
        'moe_w_gate': nrm(ks[25], (DEPTH, N_EXPERTS, d, D_EXPERT), d ** -0.5),
        'moe_w_up': nrm(ks[26], (DEPTH, N_EXPERTS, d, D_EXPERT), d ** -0.5),
        'moe_w_down': nrm(ks[27], (DEPTH, N_EXPERTS, D_EXPERT, d), DEEPNORM_BETA * D_EXPERT ** -0.5),
    }


def reference(x, c, ctx, c_ctx, w_mod, b_mod, ln1_g, ln1_b, ln2_g, ln2_b,
              attn_w_qkv, attn_w_o, attn_q_g, attn_k_g,
              ssd_w_in, ssd_conv_w, ssd_conv_b, ssd_dt_bias, ssd_a_log, ssd_d, ssd_norm_g, ssd_w_out,
              pool_w, pool_scale, moe_router, moe_w_gate, moe_w_up, moe_w_down):
    cos, sin = axial_rope_tables(x.shape[1])
    for i in range(DEPTH):
        last = i == DEPTH - 1
        kind, j = i % N_MIXERS, i // N_MIXERS
        sh1, sc1, g1, sh2, sc2, g2 = modulation(c, w_mod[i], b_mod[i])
        csh1, csc1, cg1, csh2, csc2, cg2 = modulation(c_ctx, w_mod[i], b_mod[i])
        h = modulate(x, sh1, sc1)
        hc = modulate(ctx, csh1, csc1)
        if kind == 0:
            y, yc = attention_mixer(h, hc, attn_w_qkv[j], attn_w_o[j], attn_q_g[j], attn_k_g[j], cos, sin, not last)
        elif kind == 1:
            y, yc = ssd_mixer(h, hc, ssd_w_in[j], ssd_conv_w[j], ssd_conv_b[j], ssd_dt_bias[j], ssd_a_log[j],
                              ssd_d[j], ssd_norm_g[j], ssd_w_out[j], not last)
        else:
            y = multiscale_pool_mixer(h, pool_w[j], pool_scale[j])
            yc = None if last else multiscale_pool_mixer(hc, pool_w[j], pool_scale[j])
        x = layer_norm(DEEPNORM_ALPHA * x + g1 * y, ln1_g[i], ln1_b[i])
        h = modulate(x, sh2, sc2)
        x = layer_norm(DEEPNORM_ALPHA * x + g2 * expert_choice_moe(h, moe_router[i], moe_w_gate[i], moe_w_up[i], moe_w_down[i]),
                       ln2_g[i], ln2_b[i])
        if not last:
            ctx = layer_norm(DEEPNORM_ALPHA * ctx + cg1 * yc, ln1_g[i], ln1_b[i])
            hc = modulate(ctx, csh2, csc2)
            ctx = layer_norm(DEEPNORM_ALPHA * ctx + cg2 * expert_choice_moe(hc, moe_router[i], moe_w_gate[i], moe_w_up[i], moe_w_down[i]),
                             ln2_g[i], ln2_b[i])
    return x
```

```python
import functools
import math

import jax
import jax.numpy as jnp
from jax import lax
from jax.experimental import pallas as pl
from jax.experimental.pallas import tpu as pltpu

F32 = jnp.float32
BF16 = jnp.bfloat16
I32 = jnp.int32

D_MODEL = 1024
DEPTH = 4
GRID_W = 64
N_HEADS = 16
N_KV_HEADS = 4
GQA_GROUP = N_HEADS // N_KV_HEADS
HEAD_DIM = 64
ROPE_HALF = HEAD_DIM // 2
ROPE_FREQS = HEAD_DIM // 4
ROPE_THETA = 10000.0
D_INNER = 2048
SSM_HEADDIM = 64
SSM_HEADS = 32
SSM_GROUPS = 4
SSM_STATE = 128
CONV_WIDTH = 5
CONV_DIM = D_INNER + 2 * SSM_GROUPS * SSM_STATE
SSD_CHUNK = 128
POOL_WINDOWS = (2, 4, 8, 16)
POOL_GROUP_DIM = D_MODEL // 4
N_EXPERTS = 16
D_EXPERT = 2048
EC_CAPACITY_FACTOR = 2
DEEPNORM_ALPHA = (2.0 * DEPTH) ** 0.25
LN_EPS = 1e-5
RMS_EPS = 1e-6

LANES = 128
SUBLANES = 8
VMEM_LIMIT = 56 * 1024 * 1024

ROW_TILE = 512
SSD_TILE = 256
Q_TILE = 256
K_CHUNK = 512
MOE_TB = 256
MOE_CH = 128
FFN_TF = 512
SEL_LEN = 16384
HALO = 8
Q_SCALE = (HEAD_DIM ** -0.5) * math.log2(math.e)
NEG_BIG = -1e30


def _cparams(n_axes):
    return pltpu.CompilerParams(dimension_semantics=("arbitrary",) * n_axes, vmem_limit_bytes=VMEM_LIMIT)


def _dot(a, b):
    return jnp.dot(a, b, preferred_element_type=F32)


def _dot_nt(a, b):
    return lax.dot_general(a, b, (((1,), (1,)), ((), ())), preferred_element_type=F32)


def _split2(a):
    hi = a.astype(BF16)
    lo = (a - hi.astype(F32)).astype(BF16)
    return hi, lo


def _split3(a):
    hi = a.astype(BF16)
    r = a - hi.astype(F32)
    mid = r.astype(BF16)
    lo = (r - mid.astype(F32)).astype(BF16)
    return hi, mid, lo


def _silu(v):
    return v * (1.0 / (1.0 + jnp.exp(-v)))


def _ln_residual(x, y, gate, ln_g, ln_b):
    v = DEEPNORM_ALPHA * x + gate * y
    mu = jnp.mean(v, axis=-1, keepdims=True)
    d = v - mu
    var = jnp.mean(d * d, axis=-1, keepdims=True)
    return d * lax.rsqrt(var + LN_EPS) * ln_g + ln_b


def _mod_kernel(c_ref, w_ref, b_ref, o_ref):
    s = _silu(c_ref[...])
    s_hi, s_lo = _split2(s)
    w = w_ref[0]
    w_hi, w_lo = _split2(w)
    o_ref[0] = _dot(s_hi, w_hi) + _dot(s_lo, w_hi) + _dot(s_hi, w_lo) + b_ref[0]


def _modulation(cond8, w_mod, b_mod):
    depth, d, d6 = w_mod.shape
    tn = 1536
    return pl.pallas_call(
        _mod_kernel,
        grid=(depth, d6 // tn),
        in_specs=[
            pl.BlockSpec((8, d), lambda i, j: (0, 0)),
            pl.BlockSpec((1, d, tn), lambda i, j: (i, 0, j)),
            pl.BlockSpec((1, 1, tn), lambda i, j: (i, 0, j)),
        ],
        out_specs=pl.BlockSpec((1, 8, tn), lambda i, j: (i, 0, j)),
        out_shape=jax.ShapeDtypeStruct((depth, 8, d6), F32),
        compiler_params=_cparams(2),
        name="modulation",
    )(cond8, w_mod, b_mod.reshape(depth, 1, d6))


def _qkv_kernel(x_ref, mod_ref, w_ref, qg_ref, kg_ref, cos_ref, sin_ref, qT_ref, k_ref, vT_ref):
    t = x_ref.shape[1]
    x = x_ref[0]
    h = (x * (1.0 + mod_ref[0, 1:2, :]) + mod_ref[0, 0:1, :]).astype(BF16)
    acc = _dot(h, w_ref[...])
    accT = acc.T
    cos = cos_ref[...][None]
    sin = sin_ref[...][None]

    def norm_rope(tT, n_heads, g):
        t3 = tT.reshape(n_heads, HEAD_DIM, t)
        ms = jnp.mean(t3 * t3, axis=1, keepdims=True)
        t3 = t3 * lax.rsqrt(ms + RMS_EPS) * g[None]
        t1 = t3[:, :ROPE_HALF, :]
        t2 = t3[:, ROPE_HALF:, :]
        return jnp.concatenate([t1 * cos - t2 * sin, t2 * cos + t1 * sin], axis=1)

    nq = N_HEADS * HEAD_DIM
    nk = N_KV_HEADS * HEAD_DIM
    q3 = norm_rope(accT[:nq], N_HEADS, qg_ref[...]) * Q_SCALE
    qT_ref[0] = q3.reshape(nq, t).astype(BF16)
    k3 = norm_rope(accT[nq:nq + nk], N_KV_HEADS, kg_ref[...])
    k_ref[0] = k3.reshape(nk, t).T.astype(BF16)
    vT_ref[0, 0] = accT[nq + nk:].astype(BF16)


def _qkv_project(x, mod, w_bf, qg, kg, cosT, sinT):
    b, n, d = x.shape
    t = min(ROW_TILE, n)
    nq = N_HEADS * HEAD_DIM
    nk = N_KV_HEADS * HEAD_DIM
    return pl.pallas_call(
        _qkv_kernel,
        grid=(b, n // t),
        in_specs=[
            pl.BlockSpec((1, t, d), lambda i, j: (i, j, 0)),
            pl.BlockSpec((1, 8, d), lambda i, j: (i, 0, 0)),
            pl.BlockSpec((d, nq + 2 * nk), lambda i, j: (0, 0)),
            pl.BlockSpec((HEAD_DIM, 1), lambda i, j: (0, 0)),
            pl.BlockSpec((HEAD_DIM, 1), lambda i, j: (0, 0)),
            pl.BlockSpec((ROPE_HALF, t), lambda i, j: (0, j)),
            pl.BlockSpec((ROPE_HALF, t), lambda i, j: (0, j)),
        ],
        out_specs=[
            pl.BlockSpec((1, nq, t), lambda i, j: (i, 0, j)),
            pl.BlockSpec((1, t, nk), lambda i, j: (i, j, 0)),
            pl.BlockSpec((1, 1, nk, t), lambda i, j: (i, j, 0, 0)),
        ],
        out_shape=[
            jax.ShapeDtypeStruct((b, nq, n), BF16),
            jax.ShapeDtypeStruct((b, n, nk), BF16),
            jax.ShapeDtypeStruct((b, n // t, nk, t), BF16),
        ],
        compiler_params=_cparams(2),
        name="qkv_project",
    )(x, mod, w_bf, qg, kg, cosT, sinT)


def _flash_kernel(*refs, n_lat_chunks, tk_lat):
    if n_lat_chunks:
        qT_ref, kc_ref, vTc_ref, kl_ref, vTl_ref, o_ref, acc_ref, m_ref = refs
    else:
        qT_ref, kc_ref, vTc_ref, o_ref, acc_ref, m_ref = refs
    g = pl.program_id(1)
    tq = qT_ref.shape[2]
    qb = qT_ref[0]
    q64 = jnp.concatenate([qb[j * HEAD_DIM:(j + 1) * HEAD_DIM, :] for j in range(GQA_GROUP)], axis=1)
    q256 = jnp.concatenate([q64] * N_KV_HEADS, axis=0).astype(F32)
    row_group = lax.broadcasted_iota(I32, (N_KV_HEADS * HEAD_DIM, 1), 0) // HEAD_DIM
    qpad = jnp.where(row_group == g, q256, 0.0).astype(BF16)

    m_ref[...] = jnp.full(m_ref.shape, -jnp.inf, F32)
    acc_ref[...] = jnp.zeros(acc_ref.shape, F32)

    def step(k_blk, vT_blk):
        s = _dot(k_blk, qpad)
        m_old = m_ref[...]
        m_new = jnp.maximum(m_old, jnp.max(s, axis=0, keepdims=True))
        alpha = jnp.exp2(m_old - m_new)
        p = jnp.exp2(s - m_new).astype(BF16)
        va = jnp.concatenate([vT_blk, jnp.ones((16, vT_blk.shape[1]), BF16)], axis=0)
        acc_ref[...] = acc_ref[...] * alpha + _dot(va, p)
        m_ref[...] = m_new

    step(kc_ref[0], vTc_ref[0, 0])
    if n_lat_chunks:
        def body(i, carry):
            start = pl.multiple_of(i * tk_lat, tk_lat)
            step(kl_ref[0, pl.ds(start, tk_lat), :], vTl_ref[0, i])
            return carry
        lax.fori_loop(0, n_lat_chunks, body, 0)

    acc = acc_ref[...]
    o = acc[:HEAD_DIM] * (1.0 / acc[HEAD_DIM:HEAD_DIM + 1])
    for j in range(GQA_GROUP):
        o_ref[0, j * HEAD_DIM:(j + 1) * HEAD_DIM, :] = o[:, j * tq:(j + 1) * tq].astype(BF16)


def _flash_attention(qT, k_ctx, vT_ctx, k_lat=None, vT_lat=None):
    b, nq, n = qT.shape
    tq = min(Q_TILE, n)
    lc = k_ctx.shape[1]
    nk = N_KV_HEADS * HEAD_DIM
    gw = GQA_GROUP * HEAD_DIM
    in_specs = [
        pl.BlockSpec((1, gw, tq), lambda i, g, j: (i, g, j)),
        pl.BlockSpec((1, lc, nk), lambda i, g, j: (i, 0, 0)),
        pl.BlockSpec((1, 1, HEAD_DIM, lc), lambda i, g, j: (i, 0, g, 0)),
    ]
    args = [qT, k_ctx, vT_ctx]
    n_chunks, tk = 0, 0
    if k_lat is not None:
        n_chunks, tk = vT_lat.shape[1], vT_lat.shape[3]
        in_specs += [
            pl.BlockSpec((1, k_lat.shape[1], nk), lambda i, g, j: (i, 0, 0)),
            pl.BlockSpec((1, n_chunks, HEAD_DIM, tk), lambda i, g, j: (i, 0, g, 0)),
        ]
        args += [k_lat, vT_lat]
    return pl.pallas_call(
        functools.partial(_flash_kernel, n_lat_chunks=n_chunks, tk_lat=tk),
        grid=(b, N_KV_HEADS, n // tq),
        in_specs=in_specs,
        out_specs=pl.BlockSpec((1, gw, tq), lambda i, g, j: (i, g, j)),
        out_shape=jax.ShapeDtypeStruct((b, nq, n), BF16),
        scratch_shapes=[
            pltpu.VMEM((HEAD_DIM + 16, GQA_GROUP * tq), F32),
            pltpu.VMEM((1, GQA_GROUP * tq), F32),
        ],
        compiler_params=_cparams(3),
        name="flash_attention",
    )(*args)


def _attn_out_kernel(oT_ref, woT_ref, x_ref, mod_ref, lng_ref, lnb_ref, o_ref):
    yT = _dot(woT_ref[...], oT_ref[0])
    o_ref[0] = _ln_residual(x_ref[0], yT.T, mod_ref[0, 2:3, :], lng_ref[...], lnb_ref[...])


def _attn_out(oT, woT_bf, x, mod, ln_g, ln_b):
    b, n, d = x.shape
    t = min(ROW_TILE, n)
    return pl.pallas_call(
        _attn_out_kernel,
        grid=(b, n // t),
        in_specs=[
            pl.BlockSpec((1, d, t), lambda i, j: (i, 0, j)),
            pl.BlockSpec((d, d), lambda i, j: (0, 0)),
            pl.BlockSpec((1, t, d), lambda i, j: (i, j, 0)),
            pl.BlockSpec((1, 8, d), lambda i, j: (i, 0, 0)),
            pl.BlockSpec((1, d), lambda i, j: (0, 0)),
            pl.BlockSpec((1, d), lambda i, j: (0, 0)),
        ],
        out_specs=pl.BlockSpec((1, t, d), lambda i, j: (i, j, 0)),
        out_shape=jax.ShapeDtypeStruct((b, n, d), F32),
        compiler_params=_cparams(2),
        name="attn_out_ln",
    )(oT, woT_bf, x, mod, ln_g, ln_b)


def _pool_kernel(x_ref, xp_ref, xn_ref, mod_ref, w_ref, ps_ref, lng_ref, lnb_ref, o_ref, scr_ref, *, n_tokens):
    j = pl.program_id(1)
    nt = pl.num_programs(1)
    t = x_ref.shape[1]
    sh = mod_ref[0, 0:1, :]
    sc = mod_ref[0, 1:2, :]
    x = x_ref[0]
    h = x * (1.0 + sc) + sh
    hp = xp_ref[0] * (1.0 + sc) + sh
    hn = xn_ref[0] * (1.0 + sc) + sh
    scr_ref[0:HALO, :] = jnp.where(j > 0, hp, 0.0)
    scr_ref[HALO:HALO + t, :] = h
    scr_ref[HALO + t:HALO + t + HALO, :] = jnp.where(j < nt - 1, hn, 0.0)
    tok = lax.broadcasted_iota(I32, (t, 1), 0) + j * t
    ys = []
    for gi, win in enumerate(POOL_WINDOWS):
        cols = slice(gi * POOL_GROUP_DIM, (gi + 1) * POOL_GROUP_DIM)
        half = win // 2
        acc = scr_ref[HALO - half:HALO - half + t, cols]
        for off in range(-half + 1, win - half):
            acc = acc + scr_ref[HALO + off:HALO + off + t, cols]
        lo = jnp.maximum(tok - half, 0)
        hi = jnp.minimum(tok + (win - half), n_tokens)
        cnt = (hi - lo).astype(F32)
        pooled = acc / cnt - h[:, cols]
        ys.append(_dot(pooled.astype(BF16), w_ref[gi]))
    y = jnp.concatenate(ys, axis=1) * ps_ref[...]
    o_ref[0] = _ln_residual(x, y, mod_ref[0, 2:3, :], lng_ref[...], lnb_ref[...])


def _pool_mixer(x, mod, w_bf, pool_scale, ln_g, ln_b):
    b, n, d = x.shape
    t = min(SSD_TILE, n)
    hb = t // HALO
    nb = n // HALO
    return pl.pallas_call(
        functools.partial(_pool_kernel, n_tokens=n),
        grid=(b, n // t),
        in_specs=[
            pl.BlockSpec((1, t, d), lambda i, j: (i, j, 0)),
            pl.BlockSpec((1, HALO, d), lambda i, j: (i, jnp.maximum(j * hb - 1, 0), 0)),
            pl.BlockSpec((1, HALO, d), lambda i, j: (i, jnp.minimum((j + 1) * hb, nb - 1), 0)),
            pl.BlockSpec((1, 8, d), lambda i, j: (i, 0, 0)),
            pl.BlockSpec((4, POOL_GROUP_DIM, POOL_GROUP_DIM), lambda i, j: (0, 0, 0)),
            pl.BlockSpec((1, d), lambda i, j: (0, 0)),
            pl.BlockSpec((1, d), lambda i, j: (0, 0)),
            pl.BlockSpec((1, d), lambda i, j: (0, 0)),
        ],
        out_specs=pl.BlockSpec((1, t, d), lambda i, j: (i, j, 0)),
        out_shape=jax.ShapeDtypeStruct((b, n, d), F32),
        scratch_shapes=[pltpu.VMEM((t + 2 * HALO, d), F32)],
        compiler_params=_cparams(2),
        name="pool_mixer_ln",
    )(x, x, x, mod, w_bf, pool_scale, ln_g, ln_b)


def _ssd_in_kernel(x_ref, mod_ref, wz_ref, wx_ref, wdt_ref, z_ref, xbc_ref, dt_ref):
    x = x_ref[0]
    h = (x * (1.0 + mod_ref[0, 0, 1:2, :]) + mod_ref[0, 0, 0:1, :]).astype(BF16)
    z_ref[0] = _dot(h, wz_ref[...])
    xbc_ref[0] = _dot(h, wx_ref[...])
    dt_ref[0] = _dot(h, wdt_ref[...])


def _ssd_in_proj(xcat, modcat, wz, wx, wdt):
    b, n, d = xcat.shape
    t = SSD_TILE
    return pl.pallas_call(
        _ssd_in_kernel,
        grid=(b, n // t),
        in_specs=[
            pl.BlockSpec((1, t, d), lambda i, j: (i, j, 0)),
            pl.BlockSpec((1, 1, 8, d), lambda i, j: (i, jnp.minimum(j, 1), 0, 0)),
            pl.BlockSpec((d, D_INNER), lambda i, j: (0, 0)),
            pl.BlockSpec((d, CONV_DIM), lambda i, j: (0, 0)),
            pl.BlockSpec((d, LANES), lambda i, j: (0, 0)),
        ],
        out_specs=[
            pl.BlockSpec((1, t, D_INNER), lambda i, j: (i, j, 0)),
            pl.BlockSpec((1, t, CONV_DIM), lambda i, j: (i, j, 0)),
            pl.BlockSpec((1, t, LANES), lambda i, j: (i, j, 0)),
        ],
        out_shape=[
            jax.ShapeDtypeStruct((b, n, D_INNER), F32),
            jax.ShapeDtypeStruct((b, n, CONV_DIM), F32),
            jax.ShapeDtypeStruct((b, n, LANES), F32),
        ],
        compiler_params=_cparams(2),
        name="ssd_in_proj",
    )(xcat, modcat, wz, wx, wdt)


def _ssd_conv_kernel(x_ref, xp_ref, xn_ref, w_ref, b_ref, xs_ref, bm_ref, cm_ref, scr_ref):
    j = pl.program_id(1)
    nt = pl.num_programs(1)
    t = x_ref.shape[1]
    scr_ref[0:HALO, :] = jnp.where(j > 1, xp_ref[0], 0.0)
    scr_ref[HALO:HALO + t, :] = x_ref[0]
    scr_ref[HALO + t:HALO + t + HALO, :] = jnp.where((j > 0) & (j < nt - 1), xn_ref[0], 0.0)
    pad = CONV_WIDTH // 2
    acc = b_ref[...] + w_ref[0:1, :] * scr_ref[HALO - pad:HALO - pad + t, :]
    for kk in range(1, CONV_WIDTH):
        acc = acc + w_ref[kk:kk + 1, :] * scr_ref[HALO - pad + kk:HALO - pad + kk + t, :]
    y = _silu(acc)
    nbc = SSM_GROUPS * SSM_STATE
    xs_ref[0] = y[:, :D_INNER]
    bm_ref[0] = y[:, D_INNER:D_INNER + nbc]
    cm_ref[0] = y[:, D_INNER + nbc:]


def _ssd_conv(xbc, conv_w8, conv_b):
    b, n, cdim = xbc.shape
    t = SSD_TILE
    hb = t // HALO
    nb = n // HALO
    nbc = SSM_GROUPS * SSM_STATE
    return pl.pallas_call(
        _ssd_conv_kernel,
        grid=(b, n // t),
        in_specs=[
            pl.BlockSpec((1, t, cdim), lambda i, j: (i, j, 0)),
            pl.BlockSpec((1, HALO, cdim), lambda i, j: (i, jnp.maximum(j * hb - 1, 0), 0)),
            pl.BlockSpec((1, HALO, cdim), lambda i, j: (i, jnp.minimum((j + 1) * hb, nb - 1), 0)),
            pl.BlockSpec((8, cdim), lambda i, j: (0, 0)),
            pl.BlockSpec((1, cdim), lambda i, j: (0, 0)),
        ],
        out_specs=[
            pl.BlockSpec((1, t, D_INNER), lambda i, j: (i, j, 0)),
            pl.BlockSpec((1, t, nbc), lambda i, j: (i, j, 0)),
            pl.BlockSpec((1, t, nbc), lambda i, j: (i, j, 0)),
        ],
        out_shape=[
            jax.ShapeDtypeStruct((b, n, D_INNER), F32),
            jax.ShapeDtypeStruct((b, n, nbc), F32),
            jax.ShapeDtypeStruct((b, n, nbc), F32),
        ],
        scratch_shapes=[pltpu.VMEM((t + 2 * HALO, cdim), F32)],
        compiler_params=_cparams(2),
        name="ssd_conv",
    )(xbc, xbc, xbc, conv_w8, conv_b)


def _softplus(v):
    return jnp.maximum(v, 0.0) + jnp.log(1.0 + jnp.exp(-jnp.abs(v)))


def _ssd_scan_kernel(*refs, reverse):
    if reverse:
        (xs_ref, bm_ref, cm_ref, dtc_ref, dtr_ref, arow_ref, acol_ref, brow_ref, bcol_ref,
         yprev_ref, dsk_ref, y_ref, h_ref) = refs
    else:
        (xs_ref, bm_ref, cm_ref, dtc_ref, dtr_ref, arow_ref, acol_ref, brow_ref, bcol_ref,
         y_ref, h_ref) = refs
    lc = SSD_CHUNK
    hg = SSM_HEADS // SSM_GROUPS
    gw = hg * SSM_HEADDIM

    @pl.when(pl.program_id(1) == 0)
    def _():
        h_ref[...] = jnp.zeros(h_ref.shape, F32)

    xs = xs_ref[0]
    bm = bm_ref[0]
    cm = cm_ref[0]
    dt_c = _softplus(dtc_ref[0] + brow_ref[...])
    dt_r = _softplus(dtr_ref[0] + bcol_ref[...])
    da_c = dt_c * arow_ref[...]
    da_r = dt_r * acol_ref[...]

    li = lax.broadcasted_iota(I32, (lc, lc), 0)
    si = lax.broadcasted_iota(I32, (lc, lc), 1)
    keep = (si >= li) if reverse else (si <= li)
    tri = jnp.where(keep, 1.0, 0.0).astype(BF16)
    triT = jnp.where((li >= si) if reverse else (li <= si), 1.0, 0.0).astype(BF16)
    c1, c2, c3 = _split3(da_c)
    cum_c = _dot(tri, c1) + _dot(tri, c2) + _dot(tri, c3)
    r1, r2, r3 = _split3(da_r)
    cum_r = _dot(r1, triT) + _dot(r2, triT) + _dot(r3, triT)
    end = 0 if reverse else lc - 1
    cum_end = cum_c[end:end + 1, :]

    hrow = lax.broadcasted_iota(I32, (SSM_HEADS, D_INNER), 0)
    hcol = lax.broadcasted_iota(I32, (SSM_HEADS, D_INNER), 1) // SSM_HEADDIM
    expand = jnp.where(hrow == hcol, 1.0, 0.0).astype(BF16)

    def expand_heads(v):
        v1, v2 = _split2(v)
        return _dot(v1, expand) + _dot(v2, expand)

    e_in = expand_heads(jnp.exp(cum_c))
    w_end = expand_heads(jnp.exp(cum_end - cum_c) * dt_c)
    cd = e_in[end:end + 1, :]

    lane = lax.broadcasted_iota(I32, (lc, LANES), 1)
    xs_bf = xs.astype(BF16)
    y_parts = []
    for g in range(SSM_GROUPS):
        cm_g = cm[:, g * SSM_STATE:(g + 1) * SSM_STATE].astype(BF16)
        bm_g = bm[:, g * SSM_STATE:(g + 1) * SSM_STATE]
        cb = _dot_nt(cm_g, bm_g.astype(BF16))
        h_g = h_ref[:, g * gw:(g + 1) * gw]
        y_off = _dot(cm_g, h_g.astype(BF16)) * e_in[:, g * gw:(g + 1) * gw]
        pair_out = []
        for pr in range(hg // 2):
            xs_pair = xs_bf[:, g * gw + pr * LANES:g * gw + (pr + 1) * LANES]
            outs = []
            for hh in (g * hg + 2 * pr, g * hg + 2 * pr + 1):
                diff = cum_c[:, hh:hh + 1] - cum_r[hh:hh + 1, :]
                dec = jnp.exp(jnp.where(keep, diff, NEG_BIG))
                mix = cb * dec * dt_r[hh:hh + 1, :]
                outs.append(_dot(mix.astype(BF16), xs_pair))
            pair_out.append(jnp.where(lane < SSM_HEADDIM, outs[0], outs[1]))
        y_parts.append(jnp.concatenate(pair_out, axis=1) + y_off)
        wx = (w_end[:, g * gw:(g + 1) * gw] * xs[:, g * gw:(g + 1) * gw]).astype(BF16)
        h_ref[:, g * gw:(g + 1) * gw] = h_g * cd[:, g * gw:(g + 1) * gw] + _dot(bm_g.T.astype(BF16), wx)
    y = jnp.concatenate(y_parts, axis=1)
    if reverse:
        y = y + yprev_ref[0] + dsk_ref[...] * xs
    y_ref[0] = y


def _ssd_scan(xs, bm, cm, dt_c, dt_r, a, dtb, reverse, yprev=None, dsk=None):
    b, n, _ = xs.shape
    lc = SSD_CHUNK
    nch = n // lc
    nctx = SSD_TILE // lc
    nbc = SSM_GROUPS * SSM_STATE
    if reverse:
        def cidx(j):
            return jnp.where(j < nctx, nctx - 1 - j, nch + nctx - 1 - j)
    else:
        def cidx(j):
            return j
    in_specs = [
        pl.BlockSpec((1, lc, D_INNER), lambda i, j: (i, cidx(j), 0)),
        pl.BlockSpec((1, lc, nbc), lambda i, j: (i, cidx(j), 0)),
        pl.BlockSpec((1, lc, nbc), lambda i, j: (i, cidx(j), 0)),
        pl.BlockSpec((1, lc, SSM_HEADS), lambda i, j: (i, cidx(j), 0)),
        pl.BlockSpec((1, SSM_HEADS, lc), lambda i, j: (i, 0, cidx(j))),
        pl.BlockSpec((1, SSM_HEADS), lambda i, j: (0, 0)),
        pl.BlockSpec((SSM_HEADS, 1), lambda i, j: (0, 0)),
        pl.BlockSpec((1, SSM_HEADS), lambda i, j: (0, 0)),
        pl.BlockSpec((SSM_HEADS, 1), lambda i, j: (0, 0)),
    ]
    args = [xs, bm, cm, dt_c, dt_r, a.reshape(1, -1), a.reshape(-1, 1), dtb.reshape(1, -1), dtb.reshape(-1, 1)]
    if reverse:
        in_specs += [
            pl.BlockSpec((1, lc, D_INNER), lambda i, j: (i, cidx(j), 0)),
            pl.BlockSpec((1, D_INNER), lambda i, j: (0, 0)),
        ]
        args += [yprev, dsk]
    return pl.pallas_call(
        functools.partial(_ssd_scan_kernel, reverse=reverse),
        grid=(b, nch),
        in_specs=in_specs,
        out_specs=pl.BlockSpec((1, lc, D_INNER), lambda i, j: (i, cidx(j), 0)),
        out_shape=jax.ShapeDtypeStruct((b, n, D_INNER), F32),
        scratch_shapes=[pltpu.VMEM((SSM_STATE, D_INNER), F32)],
        compiler_params=_cparams(2),
        name="ssd_scan_bwd" if reverse else "ssd_scan_fwd",
    )(*args)


def _ssd_out_kernel(y_ref, z_ref, ng_ref, w_ref, x_ref, mod_ref, lng_ref, lnb_ref, o_ref):
    y = y_ref[0] * _silu(z_ref[0])
    ms = jnp.mean(y * y, axis=-1, keepdims=True)
    yn = (y * lax.rsqrt(ms + RMS_EPS) * ng_ref[...]).astype(BF16)
    out = _dot(yn, w_ref[...])
    o_ref[0] = _ln_residual(x_ref[0], out, mod_ref[0, 0, 2:3, :], lng_ref[...], lnb_ref[...])


def _ssd_out(y, z, norm_g, w_out_bf, xcat, modcat, ln_g, ln_b):
    b, n, d = xcat.shape
    t = SSD_TILE
    return pl.pallas_call(
        _ssd_out_kernel,
        grid=(b, n // t),
        in_specs=[
            pl.BlockSpec((1, t, D_INNER), lambda i, j: (i, j, 0)),
            pl.BlockSpec((1, t, D_INNER), lambda i, j: (i, j, 0)),
            pl.BlockSpec((1, D_INNER), lambda i, j: (0, 0)),
            pl.BlockSpec((D_INNER, d), lambda i, j: (0, 0)),
            pl.BlockSpec((1, t, d), lambda i, j: (i, j, 0)),
            pl.BlockSpec((1, 1, 8, d), lambda i, j: (i, jnp.minimum(j, 1), 0, 0)),
            pl.BlockSpec((1, d), lambda i, j: (0, 0)),
            pl.BlockSpec((1, d), lambda i, j: (0, 0)),
        ],
        out_specs=pl.BlockSpec((1, t, d), lambda i, j: (i, j, 0)),
        out_shape=jax.ShapeDtypeStruct((b, n, d), F32),
        compiler_params=_cparams(2),
        name="ssd_out_ln",
    )(y, z, norm_g, w_out_bf, xcat, modcat, ln_g, ln_b)


def _router_kernel(x_ref, mod_ref, wrT_ref, h_ref, affT_ref):
    x = x_ref[0]
    h = x * (1.0 + mod_ref[0, 4:5, :]) + mod_ref[0, 3:4, :]
    h_ref[0] = h.astype(BF16)
    h1, h2, h3 = _split3(h)
    w1, w2, w3 = _split3(wrT_ref[...])
    lt = (_dot_nt(w1, h1) + _dot_nt(w1, h2) + _dot_nt(w2, h1)
          + _dot_nt(w2, h2) + _dot_nt(w1, h3) + _dot_nt(w3, h1))
    m = jnp.max(lt, axis=0, keepdims=True)
    e = jnp.exp(lt - m)
    affT_ref[0] = e / jnp.sum(e, axis=0, keepdims=True)


def _router(x, mod, wrT):
    b, n, d = x.shape
    t = min(ROW_TILE, n)
    return pl.pallas_call(
        _router_kernel,
        grid=(b, n // t),
        in_specs=[
            pl.BlockSpec((1, t, d), lambda i, j: (i, j, 0)),
            pl.BlockSpec((1, 8, d), lambda i, j: (i, 0, 0)),
            pl.BlockSpec((N_EXPERTS, d), lambda i, j: (0, 0)),
        ],
        out_specs=[
            pl.BlockSpec((1, t, d), lambda i, j: (i, j, 0)),
            pl.BlockSpec((1, N_EXPERTS, t), lambda i, j: (i, 0, j)),
        ],
        out_shape=[
            jax.ShapeDtypeStruct((b, n, d), BF16),
            jax.ShapeDtypeStruct((b, N_EXPERTS, n), F32),
        ],
        compiler_params=_cparams(2),
        name="moe_router",
    )(x, mod, wrT)


def _select_kernel(aff_ref, pos_ref, *, k):
    a = aff_ref[0]
    ne, r, _ = a.shape
    bits = pltpu.bitcast(a, I32)

    def count(mask):
        c = jnp.sum(jnp.where(mask, 1.0, 0.0), axis=1, keepdims=True)
        return jnp.sum(c, axis=2, keepdims=True)

    def body(i, thr):
        cand = thr | jnp.left_shift(jnp.int32(1), 30 - i)
        return jnp.where(count(bits >= cand) >= k, cand, thr)

    thr = lax.fori_loop(0, 31, body, jnp.zeros((ne, 1, 1), I32))
    gt = bits > thr
    eq = bits == thr
    need = k - count(gt)

    ci = lax.broadcasted_iota(I32, (LANES, LANES), 0)
    cj = lax.broadcasted_iota(I32, (LANES, LANES), 1)
    before = jnp.where(ci < cj, 1.0, 0.0).astype(BF16)
    ones = jnp.ones((LANES, LANES), BF16)
    ri = lax.broadcasted_iota(I32, (r, r), 0)
    rj = lax.broadcasted_iota(I32, (r, r), 1)
    rows_before = jnp.where(rj < ri, 1.0, 0.0).astype(BF16)

    def exclusive_prefix(flags):
        fb = flags.reshape(ne * r, LANES).astype(BF16)
        within = _dot(fb, before).reshape(ne, r, LANES)
        tot = _dot(fb, ones).reshape(ne, r, LANES).astype(BF16)
        rows = jnp.stack([_dot(rows_before, tot[e]) for e in range(ne)], axis=0)
        return within + rows

    eq_rank = exclusive_prefix(jnp.where(eq, 1.0, 0.0))
    sel = gt | (eq & (eq_rank < need))
    pos = exclusive_prefix(jnp.where(sel, 1.0, 0.0))
    pos_ref[0] = jnp.where(sel, pos, -1.0)


def _select(affT, k):
    b, ne, n = affT.shape
    if n < SEL_LEN:
        affT = jnp.pad(affT, ((0, 0), (0, 0), (0, SEL_LEN - n)), constant_values=-1.0)
    assert affT.shape[2] == SEL_LEN
    r = SEL_LEN // LANES
    pos = pl.pallas_call(
        functools.partial(_select_kernel, k=k),
        grid=(b,),
        in_specs=[pl.BlockSpec((1, ne, r, LANES), lambda i: (i, 0, 0, 0))],
        out_specs=pl.BlockSpec((1, ne, r, LANES), lambda i: (i, 0, 0, 0)),
        out_shape=jax.ShapeDtypeStruct((b, ne, r, LANES), F32),
        compiler_params=_cparams(1),
        name="moe_select",
    )(affT.reshape(b, ne, r, LANES))
    return pos.reshape(b, ne, SEL_LEN)[:, :, :n]


def _gather_kernel(tb_ref, ch_ref, first_ref, valid_ref, h_ref, pos_ref, xe_ref, acc_ref, *, steps, ch):
    idx = (pl.program_id(0) * pl.num_programs(1) + pl.program_id(1)) * steps + pl.program_id(2)

    @pl.when(first_ref[idx] == 1)
    def _():
        acc_ref[...] = jnp.zeros(acc_ref.shape, F32)

    @pl.when(valid_ref[idx] == 1)
    def _():
        base = (ch_ref[idx] * ch).astype(F32)
        slot = lax.broadcasted_iota(I32, (ch, 1), 0).astype(F32) + base
        onehot = jnp.where(pos_ref[0, 0] == slot, 1.0, 0.0).astype(BF16)
        acc_ref[...] += _dot(onehot, h_ref[0])
        xe_ref[0, 0] = acc_ref[...].astype(BF16)


def _gather(h_bf, pos, tables, cap, tb, ch, steps):
    b, n, d = h_bf.shape
    ne = pos.shape[1]
    tb_id, ch_id, first, valid = tables
    grid_spec = pltpu.PrefetchScalarGridSpec(
        num_scalar_prefetch=4,
        grid=(b, ne, steps),
        in_specs=[
            pl.BlockSpec((1, tb, d), lambda i, e, w, tbr, chr_, fr, vr: (i, tbr[(i * ne + e) * steps + w], 0)),
            pl.BlockSpec((1, 1, 1, tb), lambda i, e, w, tbr, chr_, fr, vr: (i, e, 0, tbr[(i * ne + e) * steps + w])),
        ],
        out_specs=pl.BlockSpec((1, 1, ch, d), lambda i, e, w, tbr, chr_, fr, vr: (i, e, chr_[(i * ne + e) * steps + w], 0)),
        scratch_shapes=[pltpu.VMEM((ch, d), F32)],
    )
    return pl.pallas_call(
        functools.partial(_gather_kernel, steps=steps, ch=ch),
        grid_spec=grid_spec,
        out_shape=jax.ShapeDtypeStruct((b, ne, cap, d), BF16),
        compiler_params=_cparams(3),
        name="moe_gather",
    )(tb_id, ch_id, first, valid, h_bf, pos.reshape(b, ne, 1, n))


def _ffn_kernel(xe_ref, wg_ref, wu_ref, wd_ref, ye_ref, acc_ref):
    f = pl.program_id(2)
    x = xe_ref[0, 0]
    gate = _dot(x, wg_ref[0, 0].astype(BF16))
    up = _dot(x, wu_ref[0, 0].astype(BF16))
    hid = (_silu(gate) * up).astype(BF16)
    contrib = _dot(hid, wd_ref[0, 0].astype(BF16))

    @pl.when(f == 0)
    def _():
        acc_ref[...] = contrib

    @pl.when(f > 0)
    def _():
        acc_ref[...] += contrib

    @pl.when(f == pl.num_programs(2) - 1)
    def _():
        ye_ref[0, 0] = acc_ref[...].astype(BF16)


def _expert_ffn(xe, w_gate, w_up, w_down, layer):
    b, ne, cap, d = xe.shape
    f = w_gate.shape[3]
    tf = FFN_TF
    return pl.pallas_call(
        _ffn_kernel,
        grid=(b, ne, f // tf),
        in_specs=[
            pl.BlockSpec((1, 1, cap, d), lambda i, e, j: (i, e, 0, 0)),
            pl.BlockSpec((1, 1, d, tf), lambda i, e, j: (layer, e, 0, j)),
            pl.BlockSpec((1, 1, d, tf), lambda i, e, j: (layer, e, 0, j)),
            pl.BlockSpec((1, 1, tf, d), lambda i, e, j: (layer, e, j, 0)),
        ],
        out_specs=pl.BlockSpec((1, 1, cap, d), lambda i, e, j: (i, e, 0, 0)),
        out_shape=jax.ShapeDtypeStruct((b, ne, cap, d), BF16),
        scratch_shapes=[pltpu.VMEM((cap, d), F32)],
        compiler_params=_cparams(3),
        name="moe_ffn",
    )(xe, w_gate, w_up, w_down)


def _scatter_kernel(j0_ref, j1_ref, two_ref, x_ref, mod_ref, pos_ref, aff_ref, ye0_ref, ye1_ref,
                    lng_ref, lnb_ref, o_ref, acc_ref, *, cb):
    e = pl.program_id(2)
    ne = pl.num_programs(2)
    idx = (pl.program_id(0) * pl.num_programs(1) + pl.program_id(1)) * ne + e

    @pl.when(e == 0)
    def _():
        acc_ref[...] = jnp.zeros(acc_ref.shape, F32)

    lane = lax.broadcasted_iota(I32, (1, ne), 1)
    pos_e = jnp.sum(jnp.where(lane == e, pos_ref[0], 0.0), axis=1, keepdims=True)
    gate_e = jnp.sum(jnp.where(lane == e, aff_ref[0], 0.0), axis=1, keepdims=True)
    slot = lax.broadcasted_iota(I32, (1, cb), 1).astype(F32)

    def add_block(block_idx, ye_ref):
        onehot = jnp.where(pos_e == slot + (block_idx * cb).astype(F32), 1.0, 0.0).astype(BF16)
        acc_ref[...] += gate_e * _dot(onehot, ye_ref[0, 0])

    add_block(j0_ref[idx], ye0_ref)

    @pl.when(two_ref[idx] == 1)
    def _():
        add_block(j1_ref[idx], ye1_ref)

    @pl.when(e == ne - 1)
    def _():
        o_ref[0] = _ln_residual(x_ref[0], acc_ref[...], mod_ref[0, 5:6, :], lng_ref[...], lnb_ref[...])


def _scatter_ln(x, mod, pos_tok, aff_tok, ye, tables, ln_g, ln_b, tb, cb):
    b, n, d = x.shape
    ne, cap = ye.shape[1], ye.shape[2]
    ntb = n // tb
    j0, j1, two = tables

    def flat(i, t, e):
        return (i * ntb + t) * ne + e

    grid_spec = pltpu.PrefetchScalarGridSpec(
        num_scalar_prefetch=3,
        grid=(b, ntb, ne),
        in_specs=[
            pl.BlockSpec((1, tb, d), lambda i, t, e, a0, a1, a2: (i, t, 0)),
            pl.BlockSpec((1, 8, d), lambda i, t, e, a0, a1, a2: (i, 0, 0)),
            pl.BlockSpec((1, tb, ne), lambda i, t, e, a0, a1, a2: (i, t, 0)),
            pl.BlockSpec((1, tb, ne), lambda i, t, e, a0, a1, a2: (i, t, 0)),
            pl.BlockSpec((1, 1, cb, d), lambda i, t, e, a0, a1, a2: (i, e, a0[flat(i, t, e)], 0)),
            pl.BlockSpec((1, 1, cb, d), lambda i, t, e, a0, a1, a2: (i, e, a1[flat(i, t, e)], 0)),
            pl.BlockSpec((1, d), lambda i, t, e, a0, a1, a2: (0, 0)),
            pl.BlockSpec((1, d), lambda i, t, e, a0, a1, a2: (0, 0)),
        ],
        out_specs=pl.BlockSpec((1, tb, d), lambda i, t, e, a0, a1, a2: (i, t, 0)),
        scratch_shapes=[pltpu.VMEM((tb, d), F32)],
    )
    return pl.pallas_call(
        functools.partial(_scatter_kernel, cb=cb),
        grid_spec=grid_spec,
        out_shape=jax.ShapeDtypeStruct((b, n, d), F32),
        compiler_params=_cparams(3),
        name="moe_scatter_ln",
    )(j0, j1, two, x, mod, pos_tok, aff_tok, ye, ye, ln_g, ln_b)


def _routing_tables(pos, cap, tb, ch, cb):
    b, ne, n = pos.shape
    ntb = n // tb
    nch = cap // ch
    steps = ntb + nch
    cnt = jnp.sum((pos >= 0).reshape(b, ne, ntb, tb), axis=-1).astype(I32)
    start = jnp.cumsum(cnt, axis=-1) - cnt
    last = start + cnt - 1
    c_first = jnp.minimum(start // ch, nch - 1)
    c_last = jnp.where(cnt > 0, last // ch, c_first)
    n_pairs = jnp.where(cnt > 0, c_last - c_first + 1, 0)
    pair_start = jnp.cumsum(n_pairs, axis=-1) - n_pairs
    total = jnp.sum(n_pairs, axis=-1, keepdims=True)
    w = jnp.arange(steps, dtype=I32)
    wc = jnp.minimum(w[None, None, :], total - 1)
    pair_end = pair_start + n_pairs
    tb_id = jnp.sum((pair_end[..., None, :] <= wc[..., :, None]).astype(I32), axis=-1)
    tb_id = jnp.minimum(tb_id, ntb - 1)
    take = lambda arr: jnp.take_along_axis(arr, tb_id, axis=-1)
    ch_id = take(c_first) + (wc - take(pair_start))
    valid = (w[None, None, :] < total).astype(I32)
    prev_ch = jnp.concatenate([jnp.full_like(ch_id[..., :1], -1), ch_id[..., :-1]], axis=-1)
    first = ((ch_id != prev_ch) & (valid == 1)).astype(I32)
    gather_tables = tuple(a.reshape(-1).astype(I32) for a in (tb_id, ch_id, first, valid))
    ncb = cap // cb
    j0 = jnp.minimum(start // cb, ncb - 1)
    j_last = jnp.where(cnt > 0, last // cb, j0)
    two = (j_last > j0).astype(I32)
    j1 = jnp.minimum(j0 + 1, ncb - 1)
    to_tbe = lambda arr: jnp.swapaxes(arr, 1, 2).reshape(-1).astype(I32)
    scatter_tables = (to_tbe(j0), to_tbe(j1), to_tbe(two))
    return gather_tables, scatter_tables, steps


def _moe_layer(x, mod, wrT, w_gate, w_up, w_down, layer, ln_g, ln_b):
    b, n, d = x.shape
    cap = EC_CAPACITY_FACTOR * n // N_EXPERTS
    tb = min(MOE_TB, n)
    ch = min(MOE_CH, cap)
    cb = min(MOE_TB, cap)
    h_bf, affT = _router(x, mod, wrT)
    pos = _select(affT, cap)
    gather_tables, scatter_tables, steps = _routing_tables(pos, cap, tb, ch, cb)
    xe = _gather(h_bf, pos, gather_tables, cap, tb, ch, steps)
    ye = _expert_ffn(xe, w_gate, w_up, w_down, layer)
    pos_tok = jnp.swapaxes(pos, 1, 2)
    aff_tok = jnp.swapaxes(affT, 1, 2)
    return _scatter_ln(x, mod, pos_tok, aff_tok, ye, scatter_tables, ln_g, ln_b, tb, cb)


def _rope_tables_T(n_tokens):
    rows = n_tokens // GRID_W
    row_idx = jnp.repeat(jnp.arange(rows, dtype=I32), GRID_W).astype(F32)
    col_idx = jnp.tile(jnp.arange(GRID_W, dtype=I32), rows).astype(F32)
    inv_freq = ROPE_THETA ** (-jnp.arange(ROPE_FREQS, dtype=F32) / ROPE_FREQS)
    ang = jnp.concatenate([inv_freq[:, None] * row_idx[None, :], inv_freq[:, None] * col_idx[None, :]], axis=0)
    return jnp.cos(ang), jnp.sin(ang)


def kernel(x, c, ctx, c_ctx, w_mod, b_mod, ln1_g, ln1_b, ln2_g, ln2_b, attn_w_qkv, attn_w_o, attn_q_g, attn_k_g, ssd_w_in, ssd_conv_w, ssd_conv_b, ssd_dt_bias, ssd_a_log, ssd_d, ssd_norm_g, ssd_w_out, pool_w, pool_scale, moe_router, moe_w_gate, moe_w_up, moe_w_down):
    b, n, d = x.shape
    lc = ctx.shape[1]
    assert d == D_MODEL and lc == SSD_TILE and b + 1 <= 8
    assert n % ROW_TILE == 0 and n % K_CHUNK == 0 and n <= SEL_LEN

    cond8 = jnp.concatenate([c, c_ctx[None, :], jnp.zeros((8 - b - 1, d), F32)], axis=0)
    mods = _modulation(cond8, w_mod, b_mod).reshape(DEPTH, 8, 6, d)
    mods = jnp.pad(mods, ((0, 0), (0, 0), (0, 2), (0, 0)))
    cosT, sinT = _rope_tables_T(n)
    cos_ctx = jnp.ones((ROPE_HALF, lc), F32)
    sin_ctx = jnp.zeros((ROPE_HALF, lc), F32)

    for i in range(DEPTH):
        last = i == DEPTH - 1
        kind, j = i % 3, i // 3
        mod_lat = mods[i, :b]
        mod_ctx = jnp.broadcast_to(mods[i, b][None], (b, 8, d))
        l1g, l1b = ln1_g[i][None, :], ln1_b[i][None, :]
        l2g, l2b = ln2_g[i][None, :], ln2_b[i][None, :]
        if kind == 0:
            w_bf = attn_w_qkv[j].astype(BF16)
            woT_bf = attn_w_o[j].T.astype(BF16)
            qg = attn_q_g[j][:, None]
            kg = attn_k_g[j][:, None]
            qT_l, k_l, vT_l = _qkv_project(x, mod_lat, w_bf, qg, kg, cosT, sinT)
            qT_c, k_c, vT_c = _qkv_project(ctx, mod_ctx, w_bf, qg, kg, cos_ctx, sin_ctx)
            oT_l = _flash_attention(qT_l, k_c, vT_c, k_l, vT_l)
            x = _attn_out(oT_l, woT_bf, x, mod_lat, l1g, l1b)
            if not last:
                oT_c = _flash_attention(qT_c, k_c, vT_c)
                ctx = _attn_out(oT_c, woT_bf, ctx, mod_ctx, l1g, l1b)
        elif kind == 1:
            xcat = jnp.concatenate([ctx, x], axis=1)
            modcat = jnp.stack([mod_ctx, mod_lat], axis=1)
            w_in = ssd_w_in[j]
            wz = w_in[:, :D_INNER].astype(BF16)
            wx = w_in[:, D_INNER:D_INNER + CONV_DIM].astype(BF16)
            wdt = jnp.pad(w_in[:, D_INNER + CONV_DIM:], ((0, 0), (0, LANES - 2 * SSM_HEADS))).astype(BF16)
            z, xbc, dt_raw = _ssd_in_proj(xcat, modcat, wz, wx, wdt)
            conv_w8 = jnp.pad(ssd_conv_w[j], ((0, 8 - CONV_WIDTH), (0, 0)))
            xs, bm, cm = _ssd_conv(xbc, conv_w8, ssd_conv_b[j][None, :])
            a = -jnp.exp(ssd_a_log[j].astype(F32))
            dtb = ssd_dt_bias[j].astype(F32)
            dsk = jnp.repeat(ssd_d[j], SSM_HEADDIM)[None, :]
            dt_f = dt_raw[:, :, :SSM_HEADS]
            dt_b = dt_raw[:, :, SSM_HEADS:2 * SSM_HEADS]
            y_f = _ssd_scan(xs, bm, cm, dt_f, jnp.swapaxes(dt_f, 1, 2), a[0], dtb[0], False)
            y = _ssd_scan(xs, bm, cm, dt_b, jnp.swapaxes(dt_b, 1, 2), a[1], dtb[1], True, y_f, dsk)
            xcat = _ssd_out(y, z, ssd_norm_g[j][None, :], ssd_w_out[j].astype(BF16), xcat, modcat, l1g, l1b)
            ctx, x = xcat[:, :lc], xcat[:, lc:]
        else:
            pw = pool_w[j].astype(BF16)
            ps = pool_scale[j][None, :]
            x = _pool_mixer(x, mod_lat, pw, ps, l1g, l1b)
            if not last:
                ctx = _pool_mixer(ctx, mod_ctx, pw, ps, l1g, l1b)
        wrT = moe_router[i].T
        x = _moe_layer(x, mod_lat, wrT, moe_w_gate, moe_w_up, moe_w_down, i, l2g, l2b)
        if not last:
            ctx = _moe_layer(ctx, mod_ctx, wrT, moe_w_gate, moe_w_up, moe_w_down, i, l2g, l2b)
    return x
```

```python
import functools
import math

import jax
import jax.numpy as jnp
from jax import lax
from jax.experimental import pallas as pl
from jax.experimental.pallas import tpu as pltpu

F32 = jnp.float32
BF16 = jnp.bfloat16
I32 = jnp.int32

D_MODEL = 1024
DEPTH = 4
GRID_W = 64
N_HEADS = 16
N_KV_HEADS = 4
GQA_GROUP = N_HEADS // N_KV_HEADS
HEAD_DIM = 64
ROPE_HALF = HEAD_DIM // 2
ROPE_FREQS = HEAD_DIM // 4
ROPE_THETA = 10000.0
D_INNER = 2048
SSM_HEADDIM = 64
SSM_HEADS = 32
SSM_GROUPS = 4
SSM_STATE = 128
CONV_WIDTH = 5
CONV_DIM = D_INNER + 2 * SSM_GROUPS * SSM_STATE
SSD_CHUNK = 128
POOL_WINDOWS = (2, 4, 8, 16)
POOL_GROUP_DIM = D_MODEL // 4
N_EXPERTS = 16
D_EXPERT = 2048
EC_CAPACITY_FACTOR = 2
DEEPNORM_ALPHA = (2.0 * DEPTH) ** 0.25
LN_EPS = 1e-5
RMS_EPS = 1e-6

LANES = 128
SUBLANES = 8
VMEM_LIMIT = 56 * 1024 * 1024

ROW_TILE = 512
SSD_TILE = 256
Q_TILE = 256
K_CHUNK = 512
MOE_TB = 256
MOE_CH = 128
FFN_TF = 512
SEL_LEN = 16384
HALO = 8
Q_SCALE = (HEAD_DIM ** -0.5) * math.log2(math.e)
NEG_BIG = -1e30
MXU_COLS = 256
SCORE_BOUND = 60.0
BF16_SLACK = 1.01


def _cparams(n_axes):
    return pltpu.CompilerParams(dimension_semantics=("arbitrary",) * n_axes, vmem_limit_bytes=VMEM_LIMIT)


def _dot(a, b):
    return jnp.dot(a, b, preferred_element_type=F32)


def _dot_nt(a, b):
    return lax.dot_general(a, b, (((1,), (1,)), ((), ())), preferred_element_type=F32)


def _split2(a):
    hi = a.astype(BF16)
    lo = (a - hi.astype(F32)).astype(BF16)
    return hi, lo


def _split3(a):
    hi = a.astype(BF16)
    r = a - hi.astype(F32)
    mid = r.astype(BF16)
    lo = (r - mid.astype(F32)).astype(BF16)
    return hi, mid, lo


def _silu(v):
    return v * (1.0 / (1.0 + jnp.exp(-v)))


def _ln_residual(x, y, gate, ln_g, ln_b):
    v = DEEPNORM_ALPHA * x + gate * y
    mu = jnp.mean(v, axis=-1, keepdims=True)
    d = v - mu
    var = jnp.mean(d * d, axis=-1, keepdims=True)
    return d * lax.rsqrt(var + LN_EPS) * ln_g + ln_b


def _mod_kernel(c_ref, w_ref, b_ref, o_ref):
    s = _silu(c_ref[...])
    s_hi, s_lo = _split2(s)
    w = w_ref[0]
    w_hi, w_lo = _split2(w)
    o_ref[0] = _dot(s_hi, w_hi) + _dot(s_lo, w_hi) + _dot(s_hi, w_lo) + b_ref[0]


def _modulation(cond8, w_mod, b_mod):
    depth, d, d6 = w_mod.shape
    tn = 1536
    return pl.pallas_call(
        _mod_kernel,
        grid=(depth, d6 // tn),
        in_specs=[
            pl.BlockSpec((8, d), lambda i, j: (0, 0)),
            pl.BlockSpec((1, d, tn), lambda i, j: (i, 0, j)),
            pl.BlockSpec((1, 1, tn), lambda i, j: (i, 0, j)),
        ],
        out_specs=pl.BlockSpec((1, 8, tn), lambda i, j: (i, 0, j)),
        out_shape=jax.ShapeDtypeStruct((depth, 8, d6), F32),
        compiler_params=_cparams(2),
        name="modulation",
    )(cond8, w_mod, b_mod.reshape(depth, 1, d6))


def _qkv_kernel(x_ref, mod_ref, w_ref, qg_ref, kg_ref, cos_ref, sin_ref, qT_ref, k_ref, vT_ref):
    t = x_ref.shape[1]
    x = x_ref[0]
    h = (x * (1.0 + mod_ref[0, 1:2, :]) + mod_ref[0, 0:1, :]).astype(BF16)
    acc = _dot(h, w_ref[...])
    accT = acc.T
    cos = cos_ref[...][None]
    sin = sin_ref[...][None]

    def norm_rope(tT, n_heads, g):
        t3 = tT.reshape(n_heads, HEAD_DIM, t)
        ms = jnp.mean(t3 * t3, axis=1, keepdims=True)
        t3 = t3 * lax.rsqrt(ms + RMS_EPS) * g[None]
        t1 = t3[:, :ROPE_HALF, :]
        t2 = t3[:, ROPE_HALF:, :]
        return jnp.concatenate([t1 * cos - t2 * sin, t2 * cos + t1 * sin], axis=1)

    nq = N_HEADS * HEAD_DIM
    nk = N_KV_HEADS * HEAD_DIM
    q3 = norm_rope(accT[:nq], N_HEADS, qg_ref[...]) * Q_SCALE
    qT_ref[0] = q3.reshape(nq, t).astype(BF16)
    k3 = norm_rope(accT[nq:nq + nk], N_KV_HEADS, kg_ref[...])
    k_ref[0] = k3.reshape(nk, t).T.astype(BF16)
    vT_ref[0, 0] = accT[nq + nk:].astype(BF16)


def _qkv_project(x, mod, w_bf, qg, kg, cosT, sinT):
    b, n, d = x.shape
    t = min(ROW_TILE, n)
    nq = N_HEADS * HEAD_DIM
    nk = N_KV_HEADS * HEAD_DIM
    return pl.pallas_call(
        _qkv_kernel,
        grid=(b, n // t),
        in_specs=[
            pl.BlockSpec((1, t, d), lambda i, j: (i, j, 0)),
            pl.BlockSpec((1, 8, d), lambda i, j: (i, 0, 0)),
            pl.BlockSpec((d, nq + 2 * nk), lambda i, j: (0, 0)),
            pl.BlockSpec((HEAD_DIM, 1), lambda i, j: (0, 0)),
            pl.BlockSpec((HEAD_DIM, 1), lambda i, j: (0, 0)),
            pl.BlockSpec((ROPE_HALF, t), lambda i, j: (0, j)),
            pl.BlockSpec((ROPE_HALF, t), lambda i, j: (0, j)),
        ],
        out_specs=[
            pl.BlockSpec((1, nq, t), lambda i, j: (i, 0, j)),
            pl.BlockSpec((1, t, nk), lambda i, j: (i, j, 0)),
            pl.BlockSpec((1, 1, nk, t), lambda i, j: (i, j, 0, 0)),
        ],
        out_shape=[
            jax.ShapeDtypeStruct((b, nq, n), BF16),
            jax.ShapeDtypeStruct((b, n, nk), BF16),
            jax.ShapeDtypeStruct((b, n // t, nk, t), BF16),
        ],
        compiler_params=_cparams(2),
        name="qkv_project",
    )(x, mod, w_bf, qg, kg, cosT, sinT)


def _flash_kernel(*refs, n_lat_chunks, tk_lat):
    if n_lat_chunks:
        bounded_ref, qT_ref, kc_ref, vTc_ref, kl_ref, vTl_ref, o_ref, acc_ref, m_ref, sa_ref, sb_ref = refs
    else:
        bounded_ref, qT_ref, kc_ref, vTc_ref, o_ref, acc_ref, m_ref, sa_ref, sb_ref = refs
    g = pl.program_id(1)
    tq = qT_ref.shape[2]
    qb = qT_ref[0]
    q64 = jnp.concatenate([qb[j * HEAD_DIM:(j + 1) * HEAD_DIM, :] for j in range(GQA_GROUP)], axis=1)
    q256 = jnp.concatenate([q64] * N_KV_HEADS, axis=0).astype(F32)
    row_group = lax.broadcasted_iota(I32, (N_KV_HEADS * HEAD_DIM, 1), 0) // HEAD_DIM
    qpad = jnp.where(row_group == g, q256, 0.0).astype(BF16)
    col_tiles = [slice(c * MXU_COLS, (c + 1) * MXU_COLS) for c in range(GQA_GROUP * tq // MXU_COLS)]

    acc_ref[...] = jnp.zeros(acc_ref.shape, F32)

    def with_ones(vT_blk):
        return jnp.concatenate([vT_blk, jnp.ones((16, vT_blk.shape[1]), BF16)], axis=0)

    def produce(s_ref, k_blk):
        for cs in col_tiles:
            s_ref[0:k_blk.shape[0], cs] = _dot(k_blk, qpad[:, cs])

    def consume(s_ref, vT_blk):
        va = with_ones(vT_blk)
        for cs in col_tiles:
            p = jnp.exp2(s_ref[0:vT_blk.shape[1], cs]).astype(BF16)
            acc_ref[:, cs] += _dot(va, p)

    def lat_k(i):
        return kl_ref[0, pl.ds(pl.multiple_of(i * tk_lat, tk_lat), tk_lat), :]

    def run_bounded():
        produce(sa_ref, kc_ref[0])
        if n_lat_chunks == 0:
            consume(sa_ref, vTc_ref[0, 0])
            return
        produce(sb_ref, lat_k(0))
        consume(sa_ref, vTc_ref[0, 0])

        def pair(jj, carry):
            i = 2 * jj
            produce(sa_ref, lat_k(i + 1))
            consume(sb_ref, vTl_ref[0, i])
            produce(sb_ref, lat_k(i + 2))
            consume(sa_ref, vTl_ref[0, i + 1])
            return carry

        lax.fori_loop(0, (n_lat_chunks - 1) // 2, pair, 0)
        if (n_lat_chunks - 1) % 2:
            produce(sa_ref, lat_k(n_lat_chunks - 1))
            consume(sb_ref, vTl_ref[0, n_lat_chunks - 2])
            consume(sa_ref, vTl_ref[0, n_lat_chunks - 1])
        else:
            consume(sb_ref, vTl_ref[0, n_lat_chunks - 1])

    def online_step(k_blk, vT_blk):
        va = with_ones(vT_blk)
        for cs in col_tiles:
            s = _dot(k_blk, qpad[:, cs])
            m_old = m_ref[:, cs]
            m_new = jnp.maximum(m_old, jnp.max(s, axis=0, keepdims=True))
            p = jnp.exp2(s - m_new).astype(BF16)
            acc_ref[:, cs] = acc_ref[:, cs] * jnp.exp2(m_old - m_new) + _dot(va, p)
            m_ref[:, cs] = m_new

    def run_online():
        m_ref[...] = jnp.full(m_ref.shape, -jnp.inf, F32)
        online_step(kc_ref[0], vTc_ref[0, 0])
        if n_lat_chunks:
            def body(i, carry):
                online_step(lat_k(i), vTl_ref[0, i])
                return carry
            lax.fori_loop(0, n_lat_chunks, body, 0)

    pl.when(bounded_ref[0] == 1)(run_bounded)
    pl.when(bounded_ref[0] == 0)(run_online)

    acc = acc_ref[...]
    o = acc[:HEAD_DIM] * (1.0 / acc[HEAD_DIM:HEAD_DIM + 1])
    for j in range(GQA_GROUP):
        o_ref[0, j * HEAD_DIM:(j + 1) * HEAD_DIM, :] = o[:, j * tq:(j + 1) * tq].astype(BF16)


def _flash_attention(bounded, qT, k_ctx, vT_ctx, k_lat=None, vT_lat=None):
    b, nq, n = qT.shape
    tq = min(Q_TILE, n)
    lc = k_ctx.shape[1]
    nk = N_KV_HEADS * HEAD_DIM
    gw = GQA_GROUP * HEAD_DIM
    in_specs = [
        pl.BlockSpec((1, gw, tq), lambda i, g, j, fl: (i, g, j)),
        pl.BlockSpec((1, lc, nk), lambda i, g, j, fl: (i, 0, 0)),
        pl.BlockSpec((1, 1, HEAD_DIM, lc), lambda i, g, j, fl: (i, 0, g, 0)),
    ]
    args = [qT, k_ctx, vT_ctx]
    n_chunks, tk = 0, 0
    if k_lat is not None:
        n_chunks, tk = vT_lat.shape[1], vT_lat.shape[3]
        in_specs += [
            pl.BlockSpec((1, k_lat.shape[1], nk), lambda i, g, j, fl: (i, 0, 0)),
            pl.BlockSpec((1, n_chunks, HEAD_DIM, tk), lambda i, g, j, fl: (i, 0, g, 0)),
        ]
        args += [k_lat, vT_lat]
    grid_spec = pltpu.PrefetchScalarGridSpec(
        num_scalar_prefetch=1,
        grid=(b, N_KV_HEADS, n // tq),
        in_specs=in_specs,
        out_specs=pl.BlockSpec((1, gw, tq), lambda i, g, j, fl: (i, g, j)),
        scratch_shapes=[
            pltpu.VMEM((HEAD_DIM + 16, GQA_GROUP * tq), F32),
            pltpu.VMEM((1, GQA_GROUP * tq), F32),
            pltpu.VMEM((max(tk, lc), GQA_GROUP * tq), F32),
            pltpu.VMEM((max(tk, lc), GQA_GROUP * tq), F32),
        ],
    )
    return pl.pallas_call(
        functools.partial(_flash_kernel, n_lat_chunks=n_chunks, tk_lat=tk),
        grid_spec=grid_spec,
        out_shape=jax.ShapeDtypeStruct((b, nq, n), BF16),
        compiler_params=_cparams(3),
        name="flash_attention",
    )(bounded, *args)


def _scores_bounded(q_g, k_g):
    bound = HEAD_DIM * jnp.max(jnp.abs(q_g)) * jnp.max(jnp.abs(k_g)) * (Q_SCALE * BF16_SLACK)
    return (bound <= SCORE_BOUND).astype(I32).reshape(1)


def _attn_out_kernel(oT_ref, woT_ref, x_ref, mod_ref, lng_ref, lnb_ref, o_ref):
    yT = _dot(woT_ref[...], oT_ref[0])
    o_ref[0] = _ln_residual(x_ref[0], yT.T, mod_ref[0, 2:3, :], lng_ref[...], lnb_ref[...])


def _attn_out(oT, woT_bf, x, mod, ln_g, ln_b):
    b, n, d = x.shape
    t = min(ROW_TILE, n)
    return pl.pallas_call(
        _attn_out_kernel,
        grid=(b, n // t),
        in_specs=[
            pl.BlockSpec((1, d, t), lambda i, j: (i, 0, j)),
            pl.BlockSpec((d, d), lambda i, j: (0, 0)),
            pl.BlockSpec((1, t, d), lambda i, j: (i, j, 0)),
            pl.BlockSpec((1, 8, d), lambda i, j: (i, 0, 0)),
            pl.BlockSpec((1, d), lambda i, j: (0, 0)),
            pl.BlockSpec((1, d), lambda i, j: (0, 0)),
        ],
        out_specs=pl.BlockSpec((1, t, d), lambda i, j: (i, j, 0)),
        out_shape=jax.ShapeDtypeStruct((b, n, d), F32),
        compiler_params=_cparams(2),
        name="attn_out_ln",
    )(oT, woT_bf, x, mod, ln_g, ln_b)


def _pool_kernel(x_ref, xp_ref, xn_ref, mod_ref, w_ref, ps_ref, lng_ref, lnb_ref, o_ref, scr_ref, *, n_tokens):
    j = pl.program_id(1)
    nt = pl.num_programs(1)
    t = x_ref.shape[1]
    sh = mod_ref[0, 0:1, :]
    sc = mod_ref[0, 1:2, :]
    x = x_ref[0]
    h = x * (1.0 + sc) + sh
    hp = xp_ref[0] * (1.0 + sc) + sh
    hn = xn_ref[0] * (1.0 + sc) + sh
    scr_ref[0:HALO, :] = jnp.where(j > 0, hp, 0.0)
    scr_ref[HALO:HALO + t, :] = h
    scr_ref[HALO + t:HALO + t + HALO, :] = jnp.where(j < nt - 1, hn, 0.0)
    tok = lax.broadcasted_iota(I32, (t, 1), 0) + j * t
    ys = []
    for gi, win in enumerate(POOL_WINDOWS):
        cols = slice(gi * POOL_GROUP_DIM, (gi + 1) * POOL_GROUP_DIM)
        half = win // 2
        acc = scr_ref[HALO - half:HALO - half + t, cols]
        for off in range(-half + 1, win - half):
            acc = acc + scr_ref[HALO + off:HALO + off + t, cols]
        lo = jnp.maximum(tok - half, 0)
        hi = jnp.minimum(tok + (win - half), n_tokens)
        cnt = (hi - lo).astype(F32)
        pooled = acc / cnt - h[:, cols]
        ys.append(_dot(pooled.astype(BF16), w_ref[gi]))
    y = jnp.concatenate(ys, axis=1) * ps_ref[...]
    o_ref[0] = _ln_residual(x, y, mod_ref[0, 2:3, :], lng_ref[...], lnb_ref[...])


def _pool_mixer(x, mod, w_bf, pool_scale, ln_g, ln_b):
    b, n, d = x.shape
    t = min(SSD_TILE, n)
    hb = t // HALO
    nb = n // HALO
    return pl.pallas_call(
        functools.partial(_pool_kernel, n_tokens=n),
        grid=(b, n // t),
        in_specs=[
            pl.BlockSpec((1, t, d), lambda i, j: (i, j, 0)),
            pl.BlockSpec((1, HALO, d), lambda i, j: (i, jnp.maximum(j * hb - 1, 0), 0)),
            pl.BlockSpec((1, HALO, d), lambda i, j: (i, jnp.minimum((j + 1) * hb, nb - 1), 0)),
            pl.BlockSpec((1, 8, d), lambda i, j: (i, 0, 0)),
            pl.BlockSpec((4, POOL_GROUP_DIM, POOL_GROUP_DIM), lambda i, j: (0, 0, 0)),
            pl.BlockSpec((1, d), lambda i, j: (0, 0)),
            pl.BlockSpec((1, d), lambda i, j: (0, 0)),
            pl.BlockSpec((1, d), lambda i, j: (0, 0)),
        ],
        out_specs=pl.BlockSpec((1, t, d), lambda i, j: (i, j, 0)),
        out_shape=jax.ShapeDtypeStruct((b, n, d), F32),
        scratch_shapes=[pltpu.VMEM((t + 2 * HALO, d), F32)],
        compiler_params=_cparams(2),
        name="pool_mixer_ln",
    )(x, x, x, mod, w_bf, pool_scale, ln_g, ln_b)


def _ssd_in_kernel(x_ref, mod_ref, wz_ref, wx_ref, wdt_ref, z_ref, xbc_ref, dt_ref):
    x = x_ref[0]
    h = (x * (1.0 + mod_ref[0, 0, 1:2, :]) + mod_ref[0, 0, 0:1, :]).astype(BF16)
    z_ref[0] = _dot(h, wz_ref[...])
    xbc_ref[0] = _dot(h, wx_ref[...])
    dt_ref[0] = _dot(h, wdt_ref[...])


def _ssd_in_proj(xcat, modcat, wz, wx, wdt):
    b, n, d = xcat.shape
    t = SSD_TILE
    return pl.pallas_call(
        _ssd_in_kernel,
        grid=(b, n // t),
        in_specs=[
            pl.BlockSpec((1, t, d), lambda i, j: (i, j, 0)),
            pl.BlockSpec((1, 1, 8, d), lambda i, j: (i, jnp.minimum(j, 1), 0, 0)),
            pl.BlockSpec((d, D_INNER), lambda i, j: (0, 0)),
            pl.BlockSpec((d, CONV_DIM), lambda i, j: (0, 0)),
            pl.BlockSpec((d, LANES), lambda i, j: (0, 0)),
        ],
        out_specs=[
            pl.BlockSpec((1, t, D_INNER), lambda i, j: (i, j, 0)),
            pl.BlockSpec((1, t, CONV_DIM), lambda i, j: (i, j, 0)),
            pl.BlockSpec((1, t, LANES), lambda i, j: (i, j, 0)),
        ],
        out_shape=[
            jax.ShapeDtypeStruct((b, n, D_INNER), F32),
            jax.ShapeDtypeStruct((b, n, CONV_DIM), F32),
            jax.ShapeDtypeStruct((b, n, LANES), F32),
        ],
        compiler_params=_cparams(2),
        name="ssd_in_proj",
    )(xcat, modcat, wz, wx, wdt)


def _ssd_conv_kernel(x_ref, xp_ref, xn_ref, w_ref, b_ref, xs_ref, bm_ref, cm_ref, scr_ref):
    j = pl.program_id(1)
    nt = pl.num_programs(1)
    t = x_ref.shape[1]
    scr_ref[0:HALO, :] = jnp.where(j > 1, xp_ref[0], 0.0)
    scr_ref[HALO:HALO + t, :] = x_ref[0]
    scr_ref[HALO + t:HALO + t + HALO, :] = jnp.where((j > 0) & (j < nt - 1), xn_ref[0], 0.0)
    pad = CONV_WIDTH // 2
    acc = b_ref[...] + w_ref[0:1, :] * scr_ref[HALO - pad:HALO - pad + t, :]
    for kk in range(1, CONV_WIDTH):
        acc = acc + w_ref[kk:kk + 1, :] * scr_ref[HALO - pad + kk:HALO - pad + kk + t, :]
    y = _silu(acc)
    nbc = SSM_GROUPS * SSM_STATE
    xs_ref[0] = y[:, :D_INNER]
    bm_ref[0] = y[:, D_INNER:D_INNER + nbc]
    cm_ref[0] = y[:, D_INNER + nbc:]


def _ssd_conv(xbc, conv_w8, conv_b):
    b, n, cdim = xbc.shape
    t = SSD_TILE
    hb = t // HALO
    nb = n // HALO
    nbc = SSM_GROUPS * SSM_STATE
    return pl.pallas_call(
        _ssd_conv_kernel,
        grid=(b, n // t),
        in_specs=[
            pl.BlockSpec((1, t, cdim), lambda i, j: (i, j, 0)),
            pl.BlockSpec((1, HALO, cdim), lambda i, j: (i, jnp.maximum(j * hb - 1, 0), 0)),
            pl.BlockSpec((1, HALO, cdim), lambda i, j: (i, jnp.minimum((j + 1) * hb, nb - 1), 0)),
            pl.BlockSpec((8, cdim), lambda i, j: (0, 0)),
            pl.BlockSpec((1, cdim), lambda i, j: (0, 0)),
        ],
        out_specs=[
            pl.BlockSpec((1, t, D_INNER), lambda i, j: (i, j, 0)),
            pl.BlockSpec((1, t, nbc), lambda i, j: (i, j, 0)),
            pl.BlockSpec((1, t, nbc), lambda i, j: (i, j, 0)),
        ],
        out_shape=[
            jax.ShapeDtypeStruct((b, n, D_INNER), F32),
            jax.ShapeDtypeStruct((b, n, nbc), F32),
            jax.ShapeDtypeStruct((b, n, nbc), F32),
        ],
        scratch_shapes=[pltpu.VMEM((t + 2 * HALO, cdim), F32)],
        compiler_params=_cparams(2),
        name="ssd_conv",
    )(xbc, xbc, xbc, conv_w8, conv_b)


def _softplus(v):
    return jnp.maximum(v, 0.0) + jnp.log(1.0 + jnp.exp(-jnp.abs(v)))


def _ssd_scan_kernel(*refs, reverse):
    if reverse:
        (xs_ref, bm_ref, cm_ref, dtc_ref, dtr_ref, arow_ref, acol_ref, brow_ref, bcol_ref,
         yprev_ref, dsk_ref, y_ref, h_ref) = refs
    else:
        (xs_ref, bm_ref, cm_ref, dtc_ref, dtr_ref, arow_ref, acol_ref, brow_ref, bcol_ref,
         y_ref, h_ref) = refs
    lc = SSD_CHUNK
    hg = SSM_HEADS // SSM_GROUPS
    gw = hg * SSM_HEADDIM

    @pl.when(pl.program_id(1) == 0)
    def _():
        h_ref[...] = jnp.zeros(h_ref.shape, F32)

    xs = xs_ref[0]
    bm = bm_ref[0]
    cm = cm_ref[0]
    dt_c = _softplus(dtc_ref[0] + brow_ref[...])
    dt_r = _softplus(dtr_ref[0] + bcol_ref[...])
    da_c = dt_c * arow_ref[...]
    da_r = dt_r * acol_ref[...]

    li = lax.broadcasted_iota(I32, (lc, lc), 0)
    si = lax.broadcasted_iota(I32, (lc, lc), 1)
    keep = (si >= li) if reverse else (si <= li)
    tri = jnp.where(keep, 1.0, 0.0).astype(BF16)
    triT = jnp.where((li >= si) if reverse else (li <= si), 1.0, 0.0).astype(BF16)
    c1, c2, c3 = _split3(da_c)
    cum_c = _dot(tri, c1) + _dot(tri, c2) + _dot(tri, c3)
    r1, r2, r3 = _split3(da_r)
    cum_r = _dot(r1, triT) + _dot(r2, triT) + _dot(r3, triT)
    end = 0 if reverse else lc - 1
    cum_end = cum_c[end:end + 1, :]

    hrow = lax.broadcasted_iota(I32, (SSM_HEADS, D_INNER), 0)
    hcol = lax.broadcasted_iota(I32, (SSM_HEADS, D_INNER), 1) // SSM_HEADDIM
    expand = jnp.where(hrow == hcol, 1.0, 0.0).astype(BF16)

    def expand_heads(v):
        v1, v2 = _split2(v)
        return _dot(v1, expand) + _dot(v2, expand)

    e_in = expand_heads(jnp.exp(cum_c))
    w_end = expand_heads(jnp.exp(cum_end - cum_c) * dt_c)
    cd = e_in[end:end + 1, :]

    lane = lax.broadcasted_iota(I32, (lc, LANES), 1)
    xs_bf = xs.astype(BF16)
    y_parts = []
    for g in range(SSM_GROUPS):
        cm_g = cm[:, g * SSM_STATE:(g + 1) * SSM_STATE].astype(BF16)
        bm_g = bm[:, g * SSM_STATE:(g + 1) * SSM_STATE]
        cb = _dot_nt(cm_g, bm_g.astype(BF16))
        h_g = h_ref[:, g * gw:(g + 1) * gw]
        y_off = _dot(cm_g, h_g.astype(BF16)) * e_in[:, g * gw:(g + 1) * gw]
        pair_out = []
        for pr in range(hg // 2):
            xs_pair = xs_bf[:, g * gw + pr * LANES:g * gw + (pr + 1) * LANES]
            outs = []
            for hh in (g * hg + 2 * pr, g * hg + 2 * pr + 1):
                diff = cum_c[:, hh:hh + 1] - cum_r[hh:hh + 1, :]
                dec = jnp.exp(jnp.where(keep, diff, NEG_BIG))
                mix = cb * dec * dt_r[hh:hh + 1, :]
                outs.append(_dot(mix.astype(BF16), xs_pair))
            pair_out.append(jnp.where(lane < SSM_HEADDIM, outs[0], outs[1]))
        y_parts.append(jnp.concatenate(pair_out, axis=1) + y_off)
        wx = (w_end[:, g * gw:(g + 1) * gw] * xs[:, g * gw:(g + 1) * gw]).astype(BF16)
        h_ref[:, g * gw:(g + 1) * gw] = h_g * cd[:, g * gw:(g + 1) * gw] + _dot(bm_g.T.astype(BF16), wx)
    y = jnp.concatenate(y_parts, axis=1)
    if reverse:
        y = y + yprev_ref[0] + dsk_ref[...] * xs
    y_ref[0] = y


def _ssd_scan(xs, bm, cm, dt_c, dt_r, a, dtb, reverse, yprev=None, dsk=None):
    b, n, _ = xs.shape
    lc = SSD_CHUNK
    nch = n // lc
    nctx = SSD_TILE // lc
    nbc = SSM_GROUPS * SSM_STATE
    if reverse:
        def cidx(j):
            return jnp.where(j < nctx, nctx - 1 - j, nch + nctx - 1 - j)
    else:
        def cidx(j):
            return j
    in_specs = [
        pl.BlockSpec((1, lc, D_INNER), lambda i, j: (i, cidx(j), 0)),
        pl.BlockSpec((1, lc, nbc), lambda i, j: (i, cidx(j), 0)),
        pl.BlockSpec((1, lc, nbc), lambda i, j: (i, cidx(j), 0)),
        pl.BlockSpec((1, lc, SSM_HEADS), lambda i, j: (i, cidx(j), 0)),
        pl.BlockSpec((1, SSM_HEADS, lc), lambda i, j: (i, 0, cidx(j))),
        pl.BlockSpec((1, SSM_HEADS), lambda i, j: (0, 0)),
        pl.BlockSpec((SSM_HEADS, 1), lambda i, j: (0, 0)),
        pl.BlockSpec((1, SSM_HEADS), lambda i, j: (0, 0)),
        pl.BlockSpec((SSM_HEADS, 1), lambda i, j: (0, 0)),
    ]
    args = [xs, bm, cm, dt_c, dt_r, a.reshape(1, -1), a.reshape(-1, 1), dtb.reshape(1, -1), dtb.reshape(-1, 1)]
    if reverse:
        in_specs += [
            pl.BlockSpec((1, lc, D_INNER), lambda i, j: (i, cidx(j), 0)),
            pl.BlockSpec((1, D_INNER), lambda i, j: (0, 0)),
        ]
        args += [yprev, dsk]
    return pl.pallas_call(
        functools.partial(_ssd_scan_kernel, reverse=reverse),
        grid=(b, nch),
        in_specs=in_specs,
        out_specs=pl.BlockSpec((1, lc, D_INNER), lambda i, j: (i, cidx(j), 0)),
        out_shape=jax.ShapeDtypeStruct((b, n, D_INNER), F32),
        scratch_shapes=[pltpu.VMEM((SSM_STATE, D_INNER), F32)],
        compiler_params=_cparams(2),
        name="ssd_scan_bwd" if reverse else "ssd_scan_fwd",
    )(*args)


def _ssd_out_kernel(y_ref, z_ref, ng_ref, w_ref, x_ref, mod_ref, lng_ref, lnb_ref, o_ref):
    y = y_ref[0] * _silu(z_ref[0])
    ms = jnp.mean(y * y, axis=-1, keepdims=True)
    yn = (y * lax.rsqrt(ms + RMS_EPS) * ng_ref[...]).astype(BF16)
    out = _dot(yn, w_ref[...])
    o_ref[0] = _ln_residual(x_ref[0], out, mod_ref[0, 0, 2:3, :], lng_ref[...], lnb_ref[...])


def _ssd_out(y, z, norm_g, w_out_bf, xcat, modcat, ln_g, ln_b):
    b, n, d = xcat.shape
    t = SSD_TILE
    return pl.pallas_call(
        _ssd_out_kernel,
        grid=(b, n // t),
        in_specs=[
            pl.BlockSpec((1, t, D_INNER), lambda i, j: (i, j, 0)),
            pl.BlockSpec((1, t, D_INNER), lambda i, j: (i, j, 0)),
            pl.BlockSpec((1, D_INNER), lambda i, j: (0, 0)),
            pl.BlockSpec((D_INNER, d), lambda i, j: (0, 0)),
            pl.BlockSpec((1, t, d), lambda i, j: (i, j, 0)),
            pl.BlockSpec((1, 1, 8, d), lambda i, j: (i, jnp.minimum(j, 1), 0, 0)),
            pl.BlockSpec((1, d), lambda i, j: (0, 0)),
            pl.BlockSpec((1, d), lambda i, j: (0, 0)),
        ],
        out_specs=pl.BlockSpec((1, t, d), lambda i, j: (i, j, 0)),
        out_shape=jax.ShapeDtypeStruct((b, n, d), F32),
        compiler_params=_cparams(2),
        name="ssd_out_ln",
    )(y, z, norm_g, w_out_bf, xcat, modcat, ln_g, ln_b)


def _router_kernel(x_ref, mod_ref, wrT_ref, h_ref, affT_ref):
    x = x_ref[0]
    h = x * (1.0 + mod_ref[0, 4:5, :]) + mod_ref[0, 3:4, :]
    h_ref[0] = h.astype(BF16)
    h1, h2, h3 = _split3(h)
    w1, w2, w3 = _split3(wrT_ref[...])
    lt = (_dot_nt(w1, h1) + _dot_nt(w1, h2) + _dot_nt(w2, h1)
          + _dot_nt(w2, h2) + _dot_nt(w1, h3) + _dot_nt(w3, h1))
    m = jnp.max(lt, axis=0, keepdims=True)
    e = jnp.exp(lt - m)
    affT_ref[0] = e / jnp.sum(e, axis=0, keepdims=True)


def _router(x, mod, wrT):
    b, n, d = x.shape
    t = min(ROW_TILE, n)
    return pl.pallas_call(
        _router_kernel,
        grid=(b, n // t),
        in_specs=[
            pl.BlockSpec((1, t, d), lambda i, j: (i, j, 0)),
            pl.BlockSpec((1, 8, d), lambda i, j: (i, 0, 0)),
            pl.BlockSpec((N_EXPERTS, d), lambda i, j: (0, 0)),
        ],
        out_specs=[
            pl.BlockSpec((1, t, d), lambda i, j: (i, j, 0)),
            pl.BlockSpec((1, N_EXPERTS, t), lambda i, j: (i, 0, j)),
        ],
        out_shape=[
            jax.ShapeDtypeStruct((b, n, d), BF16),
            jax.ShapeDtypeStruct((b, N_EXPERTS, n), F32),
        ],
        compiler_params=_cparams(2),
        name="moe_router",
    )(x, mod, wrT)


def _select_kernel(aff_ref, pos_ref, *, k):
    a = aff_ref[0]
    ne, r, _ = a.shape
    bits = pltpu.bitcast(a, I32)

    def count(mask):
        c = jnp.sum(jnp.where(mask, 1.0, 0.0), axis=1, keepdims=True)
        return jnp.sum(c, axis=2, keepdims=True)

    def body(i, thr):
        cand = thr | jnp.left_shift(jnp.int32(1), 30 - i)
        return jnp.where(count(bits >= cand) >= k, cand, thr)

    thr = lax.fori_loop(0, 31, body, jnp.zeros((ne, 1, 1), I32))
    gt = bits > thr
    eq = bits == thr
    need = k - count(gt)

    ci = lax.broadcasted_iota(I32, (LANES, LANES), 0)
    cj = lax.broadcasted_iota(I32, (LANES, LANES), 1)
    before = jnp.where(ci < cj, 1.0, 0.0).astype(BF16)
    ones = jnp.ones((LANES, LANES), BF16)
    ri = lax.broadcasted_iota(I32, (r, r), 0)
    rj = lax.broadcasted_iota(I32, (r, r), 1)
    rows_before = jnp.where(rj < ri, 1.0, 0.0).astype(BF16)

    def exclusive_prefix(flags):
        fb = flags.reshape(ne * r, LANES).astype(BF16)
        within = _dot(fb, before).reshape(ne, r, LANES)
        tot = _dot(fb, ones).reshape(ne, r, LANES).astype(BF16)
        rows = jnp.stack([_dot(rows_before, tot[e]) for e in range(ne)], axis=0)
        return within + rows

    eq_rank = exclusive_prefix(jnp.where(eq, 1.0, 0.0))
    sel = gt | (eq & (eq_rank < need))
    pos = exclusive_prefix(jnp.where(sel, 1.0, 0.0))
    pos_ref[0] = jnp.where(sel, pos, -1.0)


def _select(affT, k):
    b, ne, n = affT.shape
    if n < SEL_LEN:
        affT = jnp.pad(affT, ((0, 0), (0, 0), (0, SEL_LEN - n)), constant_values=-1.0)
    assert affT.shape[2] == SEL_LEN
    r = SEL_LEN // LANES
    pos = pl.pallas_call(
        functools.partial(_select_kernel, k=k),
        grid=(b,),
        in_specs=[pl.BlockSpec((1, ne, r, LANES), lambda i: (i, 0, 0, 0))],
        out_specs=pl.BlockSpec((1, ne, r, LANES), lambda i: (i, 0, 0, 0)),
        out_shape=jax.ShapeDtypeStruct((b, ne, r, LANES), F32),
        compiler_params=_cparams(1),
        name="moe_select",
    )(affT.reshape(b, ne, r, LANES))
    return pos.reshape(b, ne, SEL_LEN)[:, :, :n]


def _gather_kernel(tb_ref, ch_ref, first_ref, valid_ref, h_ref, pos_ref, xe_ref, acc_ref, *, steps, ch):
    idx = (pl.program_id(0) * pl.num_programs(1) + pl.program_id(1)) * steps + pl.program_id(2)

    @pl.when(first_ref[idx] == 1)
    def _():
        acc_ref[...] = jnp.zeros(acc_ref.shape, F32)

    @pl.when(valid_ref[idx] == 1)
    def _():
        base = (ch_ref[idx] * ch).astype(F32)
        slot = lax.broadcasted_iota(I32, (ch, 1), 0).astype(F32) + base
        onehot = jnp.where(pos_ref[0, 0] == slot, 1.0, 0.0).astype(BF16)
        acc_ref[...] += _dot(onehot, h_ref[0])
        xe_ref[0, 0] = acc_ref[...].astype(BF16)


def _gather(h_bf, pos, tables, cap, tb, ch, steps):
    b, n, d = h_bf.shape
    ne = pos.shape[1]
    tb_id, ch_id, first, valid = tables
    grid_spec = pltpu.PrefetchScalarGridSpec(
        num_scalar_prefetch=4,
        grid=(b, ne, steps),
        in_specs=[
            pl.BlockSpec((1, tb, d), lambda i, e, w, tbr, chr_, fr, vr: (i, tbr[(i * ne + e) * steps + w], 0)),
            pl.BlockSpec((1, 1, 1, tb), lambda i, e, w, tbr, chr_, fr, vr: (i, e, 0, tbr[(i * ne + e) * steps + w])),
        ],
        out_specs=pl.BlockSpec((1, 1, ch, d), lambda i, e, w, tbr, chr_, fr, vr: (i, e, chr_[(i * ne + e) * steps + w], 0)),
        scratch_shapes=[pltpu.VMEM((ch, d), F32)],
    )
    return pl.pallas_call(
        functools.partial(_gather_kernel, steps=steps, ch=ch),
        grid_spec=grid_spec,
        out_shape=jax.ShapeDtypeStruct((b, ne, cap, d), BF16),
        compiler_params=_cparams(3),
        name="moe_gather",
    )(tb_id, ch_id, first, valid, h_bf, pos.reshape(b, ne, 1, n))


def _ffn_kernel(xe_ref, wg_ref, wu_ref, wd_ref, ye_ref, acc_ref):
    f = pl.program_id(2)
    x = xe_ref[0, 0]
    gate = _dot(x, wg_ref[0, 0].astype(BF16))
    up = _dot(x, wu_ref[0, 0].astype(BF16))
    hid = (_silu(gate) * up).astype(BF16)
    contrib = _dot(hid, wd_ref[0, 0].astype(BF16))

    @pl.when(f == 0)
    def _():
        acc_ref[...] = contrib

    @pl.when(f > 0)
    def _():
        acc_ref[...] += contrib

    @pl.when(f == pl.num_programs(2) - 1)
    def _():
        ye_ref[0, 0] = acc_ref[...].astype(BF16)


def _expert_ffn(xe, w_gate, w_up, w_down, layer):
    b, ne, cap, d = xe.shape
    f = w_gate.shape[3]
    tf = FFN_TF
    return pl.pallas_call(
        _ffn_kernel,
        grid=(b, ne, f // tf),
        in_specs=[
            pl.BlockSpec((1, 1, cap, d), lambda i, e, j: (i, e, 0, 0)),
            pl.BlockSpec((1, 1, d, tf), lambda i, e, j: (layer, e, 0, j)),
            pl.BlockSpec((1, 1, d, tf), lambda i, e, j: (layer, e, 0, j)),
            pl.BlockSpec((1, 1, tf, d), lambda i, e, j: (layer, e, j, 0)),
        ],
        out_specs=pl.BlockSpec((1, 1, cap, d), lambda i, e, j: (i, e, 0, 0)),
        out_shape=jax.ShapeDtypeStruct((b, ne, cap, d), BF16),
        scratch_shapes=[pltpu.VMEM((cap, d), F32)],
        compiler_params=_cparams(3),
        name="moe_ffn",
    )(xe, w_gate, w_up, w_down)


def _scatter_kernel(j0_ref, j1_ref, two_ref, x_ref, mod_ref, pos_ref, aff_ref, ye0_ref, ye1_ref,
                    lng_ref, lnb_ref, o_ref, acc_ref, *, cb):
    e = pl.program_id(2)
    ne = pl.num_programs(2)
    idx = (pl.program_id(0) * pl.num_programs(1) + pl.program_id(1)) * ne + e

    @pl.when(e == 0)
    def _():
        acc_ref[...] = jnp.zeros(acc_ref.shape, F32)

    lane = lax.broadcasted_iota(I32, (1, ne), 1)
    pos_e = jnp.sum(jnp.where(lane == e, pos_ref[0], 0.0), axis=1, keepdims=True)
    gate_e = jnp.sum(jnp.where(lane == e, aff_ref[0], 0.0), axis=1, keepdims=True)
    slot = lax.broadcasted_iota(I32, (1, cb), 1).astype(F32)

    def add_block(block_idx, ye_ref):
        onehot = jnp.where(pos_e == slot + (block_idx * cb).astype(F32), 1.0, 0.0).astype(BF16)
        acc_ref[...] += gate_e * _dot(onehot, ye_ref[0, 0])

    add_block(j0_ref[idx], ye0_ref)

    @pl.when(two_ref[idx] == 1)
    def _():
        add_block(j1_ref[idx], ye1_ref)

    @pl.when(e == ne - 1)
    def _():
        o_ref[0] = _ln_residual(x_ref[0], acc_ref[...], mod_ref[0, 5:6, :], lng_ref[...], lnb_ref[...])


def _scatter_ln(x, mod, pos_tok, aff_tok, ye, tables, ln_g, ln_b, tb, cb):
    b, n, d = x.shape
    ne, cap = ye.shape[1], ye.shape[2]
    ntb = n // tb
    j0, j1, two = tables

    def flat(i, t, e):
        return (i * ntb + t) * ne + e

    grid_spec = pltpu.PrefetchScalarGridSpec(
        num_scalar_prefetch=3,
        grid=(b, ntb, ne),
        in_specs=[
            pl.BlockSpec((1, tb, d), lambda i, t, e, a0, a1, a2: (i, t, 0)),
            pl.BlockSpec((1, 8, d), lambda i, t, e, a0, a1, a2: (i, 0, 0)),
            pl.BlockSpec((1, tb, ne), lambda i, t, e, a0, a1, a2: (i, t, 0)),
            pl.BlockSpec((1, tb, ne), lambda i, t, e, a0, a1, a2: (i, t, 0)),
            pl.BlockSpec((1, 1, cb, d), lambda i, t, e, a0, a1, a2: (i, e, a0[flat(i, t, e)], 0)),
            pl.BlockSpec((1, 1, cb, d), lambda i, t, e, a0, a1, a2: (i, e, a1[flat(i, t, e)], 0)),
            pl.BlockSpec((1, d), lambda i, t, e, a0, a1, a2: (0, 0)),
            pl.BlockSpec((1, d), lambda i, t, e, a0, a1, a2: (0, 0)),
        ],
        out_specs=pl.BlockSpec((1, tb, d), lambda i, t, e, a0, a1, a2: (i, t, 0)),
        scratch_shapes=[pltpu.VMEM((tb, d), F32)],
    )
    return pl.pallas_call(
        functools.partial(_scatter_kernel, cb=cb),
        grid_spec=grid_spec,
        out_shape=jax.ShapeDtypeStruct((b, n, d), F32),
        compiler_params=_cparams(3),
        name="moe_scatter_ln",
    )(j0, j1, two, x, mod, pos_tok, aff_tok, ye, ye, ln_g, ln_b)


def _routing_tables(pos, cap, tb, ch, cb):
    b, ne, n = pos.shape
    ntb = n // tb
    nch = cap // ch
    steps = ntb + nch
    cnt = jnp.sum((pos >= 0).reshape(b, ne, ntb, tb), axis=-1).astype(I32)
    start = jnp.cumsum(cnt, axis=-1) - cnt
    last = start + cnt - 1
    c_first = jnp.minimum(start // ch, nch - 1)
    c_last = jnp.where(cnt > 0, last // ch, c_first)
    n_pairs = jnp.where(cnt > 0, c_last - c_first + 1, 0)
    pair_start = jnp.cumsum(n_pairs, axis=-1) - n_pairs
    total = jnp.sum(n_pairs, axis=-1, keepdims=True)
    w = jnp.arange(steps, dtype=I32)
    wc = jnp.minimum(w[None, None, :], total - 1)
    pair_end = pair_start + n_pairs
    tb_id = jnp.sum((pair_end[..., None, :] <= wc[..., :, None]).astype(I32), axis=-1)
    tb_id = jnp.minimum(tb_id, ntb - 1)
    take = lambda arr: jnp.take_along_axis(arr, tb_id, axis=-1)
    ch_id = take(c_first) + (wc - take(pair_start))
    valid = (w[None, None, :] < total).astype(I32)
    prev_ch = jnp.concatenate([jnp.full_like(ch_id[..., :1], -1), ch_id[..., :-1]], axis=-1)
    first = ((ch_id != prev_ch) & (valid == 1)).astype(I32)
    gather_tables = tuple(a.reshape(-1).astype(I32) for a in (tb_id, ch_id, first, valid))
    ncb = cap // cb
    j0 = jnp.minimum(start // cb, ncb - 1)
    j_last = jnp.where(cnt > 0, last // cb, j0)
    two = (j_last > j0).astype(I32)
    j1 = jnp.minimum(j0 + 1, ncb - 1)
    to_tbe = lambda arr: jnp.swapaxes(arr, 1, 2).reshape(-1).astype(I32)
    scatter_tables = (to_tbe(j0), to_tbe(j1), to_tbe(two))
    return gather_tables, scatter_tables, steps


def _moe_layer(x, mod, wrT, w_gate, w_up, w_down, layer, ln_g, ln_b):
    b, n, d = x.shape
    cap = EC_CAPACITY_FACTOR * n // N_EXPERTS
    tb = min(MOE_TB, n)
    ch = min(MOE_CH, cap)
    cb = min(MOE_TB, cap)
    h_bf, affT = _router(x, mod, wrT)
    pos = _select(affT, cap)
    gather_tables, scatter_tables, steps = _routing_tables(pos, cap, tb, ch, cb)
    xe = _gather(h_bf, pos, gather_tables, cap, tb, ch, steps)
    ye = _expert_ffn(xe, w_gate, w_up, w_down, layer)
    pos_tok = jnp.swapaxes(pos, 1, 2)
    aff_tok = jnp.swapaxes(affT, 1, 2)
    return _scatter_ln(x, mod, pos_tok, aff_tok, ye, scatter_tables, ln_g, ln_b, tb, cb)


def _rope_tables_T(n_tokens):
    rows = n_tokens // GRID_W
    row_idx = jnp.repeat(jnp.arange(rows, dtype=I32), GRID_W).astype(F32)
    col_idx = jnp.tile(jnp.arange(GRID_W, dtype=I32), rows).astype(F32)
    inv_freq = ROPE_THETA ** (-jnp.arange(ROPE_FREQS, dtype=F32) / ROPE_FREQS)
    ang = jnp.concatenate([inv_freq[:, None] * row_idx[None, :], inv_freq[:, None] * col_idx[None, :]], axis=0)
    return jnp.cos(ang), jnp.sin(ang)


def kernel(x, c, ctx, c_ctx, w_mod, b_mod, ln1_g, ln1_b, ln2_g, ln2_b, attn_w_qkv, attn_w_o, attn_q_g, attn_k_g, ssd_w_in, ssd_conv_w, ssd_conv_b, ssd_dt_bias, ssd_a_log, ssd_d, ssd_norm_g, ssd_w_out, pool_w, pool_scale, moe_router, moe_w_gate, moe_w_up, moe_w_down):
    b, n, d = x.shape
    lc = ctx.shape[1]
    assert d == D_MODEL and lc == SSD_TILE and b + 1 <= 8
    assert n % ROW_TILE == 0 and n % K_CHUNK == 0 and n <= SEL_LEN

    cond8 = jnp.concatenate([c, c_ctx[None, :], jnp.zeros((8 - b - 1, d), F32)], axis=0)
    mods = _modulation(cond8, w_mod, b_mod).reshape(DEPTH, 8, 6, d)
    mods = jnp.pad(mods, ((0, 0), (0, 0), (0, 2), (0, 0)))
    cosT, sinT = _rope_tables_T(n)
    cos_ctx = jnp.ones((ROPE_HALF, lc), F32)
    sin_ctx = jnp.zeros((ROPE_HALF, lc), F32)

    for i in range(DEPTH):
        last = i == DEPTH - 1
        kind, j = i % 3, i // 3
        mod_lat = mods[i, :b]
        mod_ctx = jnp.broadcast_to(mods[i, b][None], (b, 8, d))
        l1g, l1b = ln1_g[i][None, :], ln1_b[i][None, :]
        l2g, l2b = ln2_g[i][None, :], ln2_b[i][None, :]
        if kind == 0:
            w_bf = attn_w_qkv[j].astype(BF16)
            woT_bf = attn_w_o[j].T.astype(BF16)
            qg = attn_q_g[j][:, None]
            kg = attn_k_g[j][:, None]
            qT_l, k_l, vT_l = _qkv_project(x, mod_lat, w_bf, qg, kg, cosT, sinT)
            qT_c, k_c, vT_c = _qkv_project(ctx, mod_ctx, w_bf, qg, kg, cos_ctx, sin_ctx)
            bounded = _scores_bounded(attn_q_g[j], attn_k_g[j])
            oT_l = _flash_attention(bounded, qT_l, k_c, vT_c, k_l, vT_l)
            x = _attn_out(oT_l, woT_bf, x, mod_lat, l1g, l1b)
            if not last:
                oT_c = _flash_attention(bounded, qT_c, k_c, vT_c)
                ctx = _attn_out(oT_c, woT_bf, ctx, mod_ctx, l1g, l1b)
        elif kind == 1:
            xcat = jnp.concatenate([ctx, x], axis=1)
            modcat = jnp.stack([mod_ctx, mod_lat], axis=1)
            w_in = ssd_w_in[j]
            wz = w_in[:, :D_INNER].astype(BF16)
            wx = w_in[:, D_INNER:D_INNER + CONV_DIM].astype(BF16)
            wdt = jnp.pad(w_in[:, D_INNER + CONV_DIM:], ((0, 0), (0, LANES - 2 * SSM_HEADS))).astype(BF16)
            z, xbc, dt_raw = _ssd_in_proj(xcat, modcat, wz, wx, wdt)
            conv_w8 = jnp.pad(ssd_conv_w[j], ((0, 8 - CONV_WIDTH), (0, 0)))
            xs, bm, cm = _ssd_conv(xbc, conv_w8, ssd_conv_b[j][None, :])
            a = -jnp.exp(ssd_a_log[j].astype(F32))
            dtb = ssd_dt_bias[j].astype(F32)
            dsk = jnp.repeat(ssd_d[j], SSM_HEADDIM)[None, :]
            dt_f = dt_raw[:, :, :SSM_HEADS]
            dt_b = dt_raw[:, :, SSM_HEADS:2 * SSM_HEADS]
            y_f = _ssd_scan(xs, bm, cm, dt_f, jnp.swapaxes(dt_f, 1, 2), a[0], dtb[0], False)
            y = _ssd_scan(xs, bm, cm, dt_b, jnp.swapaxes(dt_b, 1, 2), a[1], dtb[1], True, y_f, dsk)
            xcat = _ssd_out(y, z, ssd_norm_g[j][None, :], ssd_w_out[j].astype(BF16), xcat, modcat, l1g, l1b)
            ctx, x = xcat[:, :lc], xcat[:, lc:]
        else:
            pw = pool_w[j].astype(BF16)
            ps = pool_scale[j][None, :]
            x = _pool_mixer(x, mod_lat, pw, ps, l1g, l1b)
            if not last:
                ctx = _pool_mixer(ctx, mod_ctx, pw, ps, l1g, l1b)
        wrT = moe_router[i].T
        x = _moe_layer(x, mod_lat, wrT, moe_w_gate, moe_w_up, moe_w_down, i, l2g, l2b)
        if not last:
            ctx = _moe_layer(ctx, mod_ctx, wrT, moe_w_gate, moe_w_up, moe_w_down, i, l2g, l2b)
    return x
```

```python
import functools
import math

import jax
import jax.numpy as jnp
from jax import lax
from jax.experimental import pallas as pl
from jax.experimental.pallas import tpu as pltpu

F32 = jnp.float32
BF16 = jnp.bfloat16
I32 = jnp.int32

D_MODEL = 1024
DEPTH = 4
GRID_W = 64
N_HEADS = 16
N_KV_HEADS = 4
GQA_GROUP = N_HEADS // N_KV_HEADS
HEAD_DIM = 64
ROPE_HALF = HEAD_DIM // 2
ROPE_FREQS = HEAD_DIM // 4
ROPE_THETA = 10000.0
D_INNER = 2048
SSM_HEADDIM = 64
SSM_HEADS = 32
SSM_GROUPS = 4
SSM_STATE = 128
CONV_WIDTH = 5
CONV_DIM = D_INNER + 2 * SSM_GROUPS * SSM_STATE
SSD_CHUNK = 128
POOL_WINDOWS = (2, 4, 8, 16)
POOL_GROUP_DIM = D_MODEL // 4
N_EXPERTS = 16
D_EXPERT = 2048
EC_CAPACITY_FACTOR = 2
DEEPNORM_ALPHA = (2.0 * DEPTH) ** 0.25
LN_EPS = 1e-5
RMS_EPS = 1e-6

LANES = 128
SUBLANES = 8
VMEM_LIMIT = 56 * 1024 * 1024

ROW_TILE = 512
SSD_TILE = 256
Q_TILE = 256
K_CHUNK = 512
PAIR_UNROLL = 3
MOE_TB = 1024
GATHER_WIN = 128
SCATTER_WIN = 256
BF16_SUBLANES = 16
FFN_TF = 512
SEL_LEN = 16384
HALO = 8
Q_SCALE = (HEAD_DIM ** -0.5) * math.log2(math.e)
NEG_BIG = -1e30
MXU_COLS = 256
SCORE_BOUND = 60.0
BF16_SLACK = 1.01


def _cparams(n_axes):
    return pltpu.CompilerParams(dimension_semantics=("arbitrary",) * n_axes, vmem_limit_bytes=VMEM_LIMIT)


def _dot(a, b):
    return jnp.dot(a, b, preferred_element_type=F32)


def _dot_nt(a, b):
    return lax.dot_general(a, b, (((1,), (1,)), ((), ())), preferred_element_type=F32)


def _split2(a):
    hi = a.astype(BF16)
    lo = (a - hi.astype(F32)).astype(BF16)
    return hi, lo


def _split3(a):
    hi = a.astype(BF16)
    r = a - hi.astype(F32)
    mid = r.astype(BF16)
    lo = (r - mid.astype(F32)).astype(BF16)
    return hi, mid, lo


def _silu(v):
    return v * (1.0 / (1.0 + jnp.exp(-v)))


def _ln_residual(x, y, gate, ln_g, ln_b):
    v = DEEPNORM_ALPHA * x + gate * y
    mu = jnp.mean(v, axis=-1, keepdims=True)
    d = v - mu
    var = jnp.mean(d * d, axis=-1, keepdims=True)
    return d * lax.rsqrt(var + LN_EPS) * ln_g + ln_b


def _mod_kernel(c_ref, w_ref, b_ref, o_ref):
    s = _silu(c_ref[...])
    s_hi, s_lo = _split2(s)
    w = w_ref[0]
    w_hi, w_lo = _split2(w)
    o_ref[0] = _dot(s_hi, w_hi) + _dot(s_lo, w_hi) + _dot(s_hi, w_lo) + b_ref[0]


def _modulation(cond8, w_mod, b_mod):
    depth, d, d6 = w_mod.shape
    tn = 1536
    return pl.pallas_call(
        _mod_kernel,
        grid=(depth, d6 // tn),
        in_specs=[
            pl.BlockSpec((8, d), lambda i, j: (0, 0)),
            pl.BlockSpec((1, d, tn), lambda i, j: (i, 0, j)),
            pl.BlockSpec((1, 1, tn), lambda i, j: (i, 0, j)),
        ],
        out_specs=pl.BlockSpec((1, 8, tn), lambda i, j: (i, 0, j)),
        out_shape=jax.ShapeDtypeStruct((depth, 8, d6), F32),
        compiler_params=_cparams(2),
        name="modulation",
    )(cond8, w_mod, b_mod.reshape(depth, 1, d6))


def _qkv_kernel(x_ref, mod_ref, w_ref, qg_ref, kg_ref, cos_ref, sin_ref, qT_ref, k_ref, vT_ref):
    t = x_ref.shape[1]
    x = x_ref[0]
    h = (x * (1.0 + mod_ref[0, 1:2, :]) + mod_ref[0, 0:1, :]).astype(BF16)
    acc = _dot(h, w_ref[...])
    accT = acc.T
    cos = cos_ref[...][None]
    sin = sin_ref[...][None]

    def norm_rope(tT, n_heads, g):
        t3 = tT.reshape(n_heads, HEAD_DIM, t)
        ms = jnp.mean(t3 * t3, axis=1, keepdims=True)
        t3 = t3 * lax.rsqrt(ms + RMS_EPS) * g[None]
        t1 = t3[:, :ROPE_HALF, :]
        t2 = t3[:, ROPE_HALF:, :]
        return jnp.concatenate([t1 * cos - t2 * sin, t2 * cos + t1 * sin], axis=1)

    nq = N_HEADS * HEAD_DIM
    nk = N_KV_HEADS * HEAD_DIM
    q3 = norm_rope(accT[:nq], N_HEADS, qg_ref[...]) * Q_SCALE
    qT_ref[0] = q3.reshape(nq, t).astype(BF16)
    k3 = norm_rope(accT[nq:nq + nk], N_KV_HEADS, kg_ref[...])
    k_ref[0] = k3.reshape(nk, t).T.astype(BF16)
    vT_ref[0, 0] = accT[nq + nk:].astype(BF16)


def _qkv_project(x, mod, w_bf, qg, kg, cosT, sinT):
    b, n, d = x.shape
    t = min(ROW_TILE, n)
    nq = N_HEADS * HEAD_DIM
    nk = N_KV_HEADS * HEAD_DIM
    return pl.pallas_call(
        _qkv_kernel,
        grid=(b, n // t),
        in_specs=[
            pl.BlockSpec((1, t, d), lambda i, j: (i, j, 0)),
            pl.BlockSpec((1, 8, d), lambda i, j: (i, 0, 0)),
            pl.BlockSpec((d, nq + 2 * nk), lambda i, j: (0, 0)),
            pl.BlockSpec((HEAD_DIM, 1), lambda i, j: (0, 0)),
            pl.BlockSpec((HEAD_DIM, 1), lambda i, j: (0, 0)),
            pl.BlockSpec((ROPE_HALF, t), lambda i, j: (0, j)),
            pl.BlockSpec((ROPE_HALF, t), lambda i, j: (0, j)),
        ],
        out_specs=[
            pl.BlockSpec((1, nq, t), lambda i, j: (i, 0, j)),
            pl.BlockSpec((1, t, nk), lambda i, j: (i, j, 0)),
            pl.BlockSpec((1, 1, nk, t), lambda i, j: (i, j, 0, 0)),
        ],
        out_shape=[
            jax.ShapeDtypeStruct((b, nq, n), BF16),
            jax.ShapeDtypeStruct((b, n, nk), BF16),
            jax.ShapeDtypeStruct((b, n // t, nk, t), BF16),
        ],
        compiler_params=_cparams(2),
        name="qkv_project",
    )(x, mod, w_bf, qg, kg, cosT, sinT)


def _flash_kernel(*refs, n_lat_chunks, tk_lat):
    if n_lat_chunks:
        bounded_ref, qT_ref, kc_ref, vTc_ref, kl_ref, vTl_ref, o_ref, acc_ref, m_ref, sa_ref, sb_ref = refs
    else:
        bounded_ref, qT_ref, kc_ref, vTc_ref, o_ref, acc_ref, m_ref, sa_ref, sb_ref = refs
    g = pl.program_id(1)
    tq = qT_ref.shape[2]
    qb = qT_ref[0]
    q64 = jnp.concatenate([qb[j * HEAD_DIM:(j + 1) * HEAD_DIM, :] for j in range(GQA_GROUP)], axis=1)
    q256 = jnp.concatenate([q64] * N_KV_HEADS, axis=0).astype(F32)
    row_group = lax.broadcasted_iota(I32, (N_KV_HEADS * HEAD_DIM, 1), 0) // HEAD_DIM
    qpad = jnp.where(row_group == g, q256, 0.0).astype(BF16)
    col_tiles = [slice(c * MXU_COLS, (c + 1) * MXU_COLS) for c in range(GQA_GROUP * tq // MXU_COLS)]

    acc_ref[...] = jnp.zeros(acc_ref.shape, F32)

    def with_ones(vT_blk):
        return jnp.concatenate([vT_blk, jnp.ones((16, vT_blk.shape[1]), BF16)], axis=0)

    def produce(s_ref, k_blk):
        for cs in col_tiles:
            s_ref[0:k_blk.shape[0], cs] = _dot(k_blk, qpad[:, cs])

    def consume(s_ref, vT_blk):
        va = with_ones(vT_blk)
        for cs in col_tiles:
            p = jnp.exp2(s_ref[0:vT_blk.shape[1], cs]).astype(BF16)
            acc_ref[:, cs] += _dot(va, p)

    def lat_k(i):
        return kl_ref[0, pl.ds(pl.multiple_of(i * tk_lat, tk_lat), tk_lat), :]

    def run_bounded():
        produce(sa_ref, kc_ref[0])
        if n_lat_chunks == 0:
            consume(sa_ref, vTc_ref[0, 0])
            return
        produce(sb_ref, lat_k(0))
        consume(sa_ref, vTc_ref[0, 0])

        def pair(jj, carry):
            i = 2 * jj
            produce(sa_ref, lat_k(i + 1))
            consume(sb_ref, vTl_ref[0, i])
            produce(sb_ref, lat_k(i + 2))
            consume(sa_ref, vTl_ref[0, i + 1])
            return carry

        lax.fori_loop(0, (n_lat_chunks - 1) // 2, pair, 0, unroll=PAIR_UNROLL)
        if (n_lat_chunks - 1) % 2:
            produce(sa_ref, lat_k(n_lat_chunks - 1))
            consume(sb_ref, vTl_ref[0, n_lat_chunks - 2])
            consume(sa_ref, vTl_ref[0, n_lat_chunks - 1])
        else:
            consume(sb_ref, vTl_ref[0, n_lat_chunks - 1])

    def online_step(k_blk, vT_blk):
        va = with_ones(vT_blk)
        for cs in col_tiles:
            s = _dot(k_blk, qpad[:, cs])
            m_old = m_ref[:, cs]
            m_new = jnp.maximum(m_old, jnp.max(s, axis=0, keepdims=True))
            p = jnp.exp2(s - m_new).astype(BF16)
            acc_ref[:, cs] = acc_ref[:, cs] * jnp.exp2(m_old - m_new) + _dot(va, p)
            m_ref[:, cs] = m_new

    def run_online():
        m_ref[...] = jnp.full(m_ref.shape, -jnp.inf, F32)
        online_step(kc_ref[0], vTc_ref[0, 0])
        if n_lat_chunks:
            def body(i, carry):
                online_step(lat_k(i), vTl_ref[0, i])
                return carry
            lax.fori_loop(0, n_lat_chunks, body, 0)

    pl.when(bounded_ref[0] == 1)(run_bounded)
    pl.when(bounded_ref[0] == 0)(run_online)

    acc = acc_ref[...]
    o = acc[:HEAD_DIM] * (1.0 / acc[HEAD_DIM:HEAD_DIM + 1])
    for j in range(GQA_GROUP):
        o_ref[0, j * HEAD_DIM:(j + 1) * HEAD_DIM, :] = o[:, j * tq:(j + 1) * tq].astype(BF16)


def _flash_attention(bounded, qT, k_ctx, vT_ctx, k_lat=None, vT_lat=None):
    b, nq, n = qT.shape
    tq = min(Q_TILE, n)
    lc = k_ctx.shape[1]
    nk = N_KV_HEADS * HEAD_DIM
    gw = GQA_GROUP * HEAD_DIM
    in_specs = [
        pl.BlockSpec((1, gw, tq), lambda i, g, j, fl: (i, g, j)),
        pl.BlockSpec((1, lc, nk), lambda i, g, j, fl: (i, 0, 0)),
        pl.BlockSpec((1, 1, HEAD_DIM, lc), lambda i, g, j, fl: (i, 0, g, 0)),
    ]
    args = [qT, k_ctx, vT_ctx]
    n_chunks, tk = 0, 0
    if k_lat is not None:
        n_chunks, tk = vT_lat.shape[1], vT_lat.shape[3]
        in_specs += [
            pl.BlockSpec((1, k_lat.shape[1], nk), lambda i, g, j, fl: (i, 0, 0)),
            pl.BlockSpec((1, n_chunks, HEAD_DIM, tk), lambda i, g, j, fl: (i, 0, g, 0)),
        ]
        args += [k_lat, vT_lat]
    grid_spec = pltpu.PrefetchScalarGridSpec(
        num_scalar_prefetch=1,
        grid=(b, N_KV_HEADS, n // tq),
        in_specs=in_specs,
        out_specs=pl.BlockSpec((1, gw, tq), lambda i, g, j, fl: (i, g, j)),
        scratch_shapes=[
            pltpu.VMEM((HEAD_DIM + 16, GQA_GROUP * tq), F32),
            pltpu.VMEM((1, GQA_GROUP * tq), F32),
            pltpu.VMEM((max(tk, lc), GQA_GROUP * tq), F32),
            pltpu.VMEM((max(tk, lc), GQA_GROUP * tq), F32),
        ],
    )
    return pl.pallas_call(
        functools.partial(_flash_kernel, n_lat_chunks=n_chunks, tk_lat=tk),
        grid_spec=grid_spec,
        out_shape=jax.ShapeDtypeStruct((b, nq, n), BF16),
        compiler_params=_cparams(3),
        name="flash_attention",
    )(bounded, *args)


def _scores_bounded(q_g, k_g):
    bound = HEAD_DIM * jnp.max(jnp.abs(q_g)) * jnp.max(jnp.abs(k_g)) * (Q_SCALE * BF16_SLACK)
    return (bound <= SCORE_BOUND).astype(I32).reshape(1)


def _attn_out_kernel(oT_ref, woT_ref, x_ref, mod_ref, lng_ref, lnb_ref, o_ref):
    yT = _dot(woT_ref[...], oT_ref[0])
    o_ref[0] = _ln_residual(x_ref[0], yT.T, mod_ref[0, 2:3, :], lng_ref[...], lnb_ref[...])


def _attn_out(oT, woT_bf, x, mod, ln_g, ln_b):
    b, n, d = x.shape
    t = min(ROW_TILE, n)
    return pl.pallas_call(
        _attn_out_kernel,
        grid=(b, n // t),
        in_specs=[
            pl.BlockSpec((1, d, t), lambda i, j: (i, 0, j)),
            pl.BlockSpec((d, d), lambda i, j: (0, 0)),
            pl.BlockSpec((1, t, d), lambda i, j: (i, j, 0)),
            pl.BlockSpec((1, 8, d), lambda i, j: (i, 0, 0)),
            pl.BlockSpec((1, d), lambda i, j: (0, 0)),
            pl.BlockSpec((1, d), lambda i, j: (0, 0)),
        ],
        out_specs=pl.BlockSpec((1, t, d), lambda i, j: (i, j, 0)),
        out_shape=jax.ShapeDtypeStruct((b, n, d), F32),
        compiler_params=_cparams(2),
        name="attn_out_ln",
    )(oT, woT_bf, x, mod, ln_g, ln_b)


def _pool_kernel(x_ref, xp_ref, xn_ref, mod_ref, w_ref, ps_ref, lng_ref, lnb_ref, o_ref, scr_ref, *, n_tokens):
    j = pl.program_id(1)
    nt = pl.num_programs(1)
    t = x_ref.shape[1]
    sh = mod_ref[0, 0:1, :]
    sc = mod_ref[0, 1:2, :]
    x = x_ref[0]
    h = x * (1.0 + sc) + sh
    hp = xp_ref[0] * (1.0 + sc) + sh
    hn = xn_ref[0] * (1.0 + sc) + sh
    scr_ref[0:HALO, :] = jnp.where(j > 0, hp, 0.0)
    scr_ref[HALO:HALO + t, :] = h
    scr_ref[HALO + t:HALO + t + HALO, :] = jnp.where(j < nt - 1, hn, 0.0)
    tok = lax.broadcasted_iota(I32, (t, 1), 0) + j * t
    ys = []
    for gi, win in enumerate(POOL_WINDOWS):
        cols = slice(gi * POOL_GROUP_DIM, (gi + 1) * POOL_GROUP_DIM)
        half = win // 2
        acc = scr_ref[HALO - half:HALO - half + t, cols]
        for off in range(-half + 1, win - half):
            acc = acc + scr_ref[HALO + off:HALO + off + t, cols]
        lo = jnp.maximum(tok - half, 0)
        hi = jnp.minimum(tok + (win - half), n_tokens)
        cnt = (hi - lo).astype(F32)
        pooled = acc / cnt - h[:, cols]
        ys.append(_dot(pooled.astype(BF16), w_ref[gi]))
    y = jnp.concatenate(ys, axis=1) * ps_ref[...]
    o_ref[0] = _ln_residual(x, y, mod_ref[0, 2:3, :], lng_ref[...], lnb_ref[...])


def _pool_mixer(x, mod, w_bf, pool_scale, ln_g, ln_b):
    b, n, d = x.shape
    t = min(SSD_TILE, n)
    hb = t // HALO
    nb = n // HALO
    return pl.pallas_call(
        functools.partial(_pool_kernel, n_tokens=n),
        grid=(b, n // t),
        in_specs=[
            pl.BlockSpec((1, t, d), lambda i, j: (i, j, 0)),
            pl.BlockSpec((1, HALO, d), lambda i, j: (i, jnp.maximum(j * hb - 1, 0), 0)),
            pl.BlockSpec((1, HALO, d), lambda i, j: (i, jnp.minimum((j + 1) * hb, nb - 1), 0)),
            pl.BlockSpec((1, 8, d), lambda i, j: (i, 0, 0)),
            pl.BlockSpec((4, POOL_GROUP_DIM, POOL_GROUP_DIM), lambda i, j: (0, 0, 0)),
            pl.BlockSpec((1, d), lambda i, j: (0, 0)),
            pl.BlockSpec((1, d), lambda i, j: (0, 0)),
            pl.BlockSpec((1, d), lambda i, j: (0, 0)),
        ],
        out_specs=pl.BlockSpec((1, t, d), lambda i, j: (i, j, 0)),
        out_shape=jax.ShapeDtypeStruct((b, n, d), F32),
        scratch_shapes=[pltpu.VMEM((t + 2 * HALO, d), F32)],
        compiler_params=_cparams(2),
        name="pool_mixer_ln",
    )(x, x, x, mod, w_bf, pool_scale, ln_g, ln_b)


def _ssd_in_kernel(x_ref, mod_ref, wz_ref, wx_ref, wdt_ref, z_ref, xbc_ref, dt_ref):
    x = x_ref[0]
    h = (x * (1.0 + mod_ref[0, 0, 1:2, :]) + mod_ref[0, 0, 0:1, :]).astype(BF16)
    z_ref[0] = _dot(h, wz_ref[...])
    xbc_ref[0] = _dot(h, wx_ref[...])
    dt_ref[0] = _dot(h, wdt_ref[...])


def _ssd_in_proj(xcat, modcat, wz, wx, wdt):
    b, n, d = xcat.shape
    t = SSD_TILE
    return pl.pallas_call(
        _ssd_in_kernel,
        grid=(b, n // t),
        in_specs=[
            pl.BlockSpec((1, t, d), lambda i, j: (i, j, 0)),
            pl.BlockSpec((1, 1, 8, d), lambda i, j: (i, jnp.minimum(j, 1), 0, 0)),
            pl.BlockSpec((d, D_INNER), lambda i, j: (0, 0)),
            pl.BlockSpec((d, CONV_DIM), lambda i, j: (0, 0)),
            pl.BlockSpec((d, LANES), lambda i, j: (0, 0)),
        ],
        out_specs=[
            pl.BlockSpec((1, t, D_INNER), lambda i, j: (i, j, 0)),
            pl.BlockSpec((1, t, CONV_DIM), lambda i, j: (i, j, 0)),
            pl.BlockSpec((1, t, LANES), lambda i, j: (i, j, 0)),
        ],
        out_shape=[
            jax.ShapeDtypeStruct((b, n, D_INNER), F32),
            jax.ShapeDtypeStruct((b, n, CONV_DIM), F32),
            jax.ShapeDtypeStruct((b, n, LANES), F32),
        ],
        compiler_params=_cparams(2),
        name="ssd_in_proj",
    )(xcat, modcat, wz, wx, wdt)


def _ssd_conv_kernel(x_ref, xp_ref, xn_ref, w_ref, b_ref, xs_ref, bm_ref, cm_ref, scr_ref):
    j = pl.program_id(1)
    nt = pl.num_programs(1)
    t = x_ref.shape[1]
    scr_ref[0:HALO, :] = jnp.where(j > 1, xp_ref[0], 0.0)
    scr_ref[HALO:HALO + t, :] = x_ref[0]
    scr_ref[HALO + t:HALO + t + HALO, :] = jnp.where((j > 0) & (j < nt - 1), xn_ref[0], 0.0)
    pad = CONV_WIDTH // 2
    acc = b_ref[...] + w_ref[0:1, :] * scr_ref[HALO - pad:HALO - pad + t, :]
    for kk in range(1, CONV_WIDTH):
        acc = acc + w_ref[kk:kk + 1, :] * scr_ref[HALO - pad + kk:HALO - pad + kk + t, :]
    y = _silu(acc)
    nbc = SSM_GROUPS * SSM_STATE
    xs_ref[0] = y[:, :D_INNER]
    bm_ref[0] = y[:, D_INNER:D_INNER + nbc]
    cm_ref[0] = y[:, D_INNER + nbc:]


def _ssd_conv(xbc, conv_w8, conv_b):
    b, n, cdim = xbc.shape
    t = SSD_TILE
    hb = t // HALO
    nb = n // HALO
    nbc = SSM_GROUPS * SSM_STATE
    return pl.pallas_call(
        _ssd_conv_kernel,
        grid=(b, n // t),
        in_specs=[
            pl.BlockSpec((1, t, cdim), lambda i, j: (i, j, 0)),
            pl.BlockSpec((1, HALO, cdim), lambda i, j: (i, jnp.maximum(j * hb - 1, 0), 0)),
            pl.BlockSpec((1, HALO, cdim), lambda i, j: (i, jnp.minimum((j + 1) * hb, nb - 1), 0)),
            pl.BlockSpec((8, cdim), lambda i, j: (0, 0)),
            pl.BlockSpec((1, cdim), lambda i, j: (0, 0)),
        ],
        out_specs=[
            pl.BlockSpec((1, t, D_INNER), lambda i, j: (i, j, 0)),
            pl.BlockSpec((1, t, nbc), lambda i, j: (i, j, 0)),
            pl.BlockSpec((1, t, nbc), lambda i, j: (i, j, 0)),
        ],
        out_shape=[
            jax.ShapeDtypeStruct((b, n, D_INNER), F32),
            jax.ShapeDtypeStruct((b, n, nbc), F32),
            jax.ShapeDtypeStruct((b, n, nbc), F32),
        ],
        scratch_shapes=[pltpu.VMEM((t + 2 * HALO, cdim), F32)],
        compiler_params=_cparams(2),
        name="ssd_conv",
    )(xbc, xbc, xbc, conv_w8, conv_b)


def _softplus(v):
    return jnp.maximum(v, 0.0) + jnp.log(1.0 + jnp.exp(-jnp.abs(v)))


def _ssd_scan_kernel(*refs, reverse):
    if reverse:
        (xs_ref, bm_ref, cm_ref, dtc_ref, dtr_ref, arow_ref, acol_ref, brow_ref, bcol_ref,
         yprev_ref, dsk_ref, y_ref, h_ref) = refs
    else:
        (xs_ref, bm_ref, cm_ref, dtc_ref, dtr_ref, arow_ref, acol_ref, brow_ref, bcol_ref,
         y_ref, h_ref) = refs
    lc = SSD_CHUNK
    hg = SSM_HEADS // SSM_GROUPS
    gw = hg * SSM_HEADDIM

    @pl.when(pl.program_id(1) == 0)
    def _():
        h_ref[...] = jnp.zeros(h_ref.shape, F32)

    xs = xs_ref[0]
    bm = bm_ref[0]
    cm = cm_ref[0]
    dt_c = _softplus(dtc_ref[0] + brow_ref[...])
    dt_r = _softplus(dtr_ref[0] + bcol_ref[...])
    da_c = dt_c * arow_ref[...]
    da_r = dt_r * acol_ref[...]

    li = lax.broadcasted_iota(I32, (lc, lc), 0)
    si = lax.broadcasted_iota(I32, (lc, lc), 1)
    keep = (si >= li) if reverse else (si <= li)
    tri = jnp.where(keep, 1.0, 0.0).astype(BF16)
    triT = jnp.where((li >= si) if reverse else (li <= si), 1.0, 0.0).astype(BF16)
    c1, c2, c3 = _split3(da_c)
    cum_c = _dot(tri, c1) + _dot(tri, c2) + _dot(tri, c3)
    r1, r2, r3 = _split3(da_r)
    cum_r = _dot(r1, triT) + _dot(r2, triT) + _dot(r3, triT)
    end = 0 if reverse else lc - 1
    cum_end = cum_c[end:end + 1, :]

    hrow = lax.broadcasted_iota(I32, (SSM_HEADS, D_INNER), 0)
    hcol = lax.broadcasted_iota(I32, (SSM_HEADS, D_INNER), 1) // SSM_HEADDIM
    expand = jnp.where(hrow == hcol, 1.0, 0.0).astype(BF16)

    def expand_heads(v):
        v1, v2 = _split2(v)
        return _dot(v1, expand) + _dot(v2, expand)

    e_in = expand_heads(jnp.exp(cum_c))
    w_end = expand_heads(jnp.exp(cum_end - cum_c) * dt_c)
    cd = e_in[end:end + 1, :]

    lane = lax.broadcasted_iota(I32, (lc, LANES), 1)
    xs_bf = xs.astype(BF16)
    y_parts = []
    for g in range(SSM_GROUPS):
        cm_g = cm[:, g * SSM_STATE:(g + 1) * SSM_STATE].astype(BF16)
        bm_g = bm[:, g * SSM_STATE:(g + 1) * SSM_STATE]
        cb = _dot_nt(cm_g, bm_g.astype(BF16))
        h_g = h_ref[:, g * gw:(g + 1) * gw]
        y_off = _dot(cm_g, h_g.astype(BF16)) * e_in[:, g * gw:(g + 1) * gw]
        pair_out = []
        for pr in range(hg // 2):
            xs_pair = xs_bf[:, g * gw + pr * LANES:g * gw + (pr + 1) * LANES]
            outs = []
            for hh in (g * hg + 2 * pr, g * hg + 2 * pr + 1):
                diff = cum_c[:, hh:hh + 1] - cum_r[hh:hh + 1, :]
                dec = jnp.exp(jnp.where(keep, diff, NEG_BIG))
                mix = cb * dec * dt_r[hh:hh + 1, :]
                outs.append(_dot(mix.astype(BF16), xs_pair))
            pair_out.append(jnp.where(lane < SSM_HEADDIM, outs[0], outs[1]))
        y_parts.append(jnp.concatenate(pair_out, axis=1) + y_off)
        wx = (w_end[:, g * gw:(g + 1) * gw] * xs[:, g * gw:(g + 1) * gw]).astype(BF16)
        h_ref[:, g * gw:(g + 1) * gw] = h_g * cd[:, g * gw:(g + 1) * gw] + _dot(bm_g.T.astype(BF16), wx)
    y = jnp.concatenate(y_parts, axis=1)
    if reverse:
        y = y + yprev_ref[0] + dsk_ref[...] * xs
    y_ref[0] = y


def _ssd_scan(xs, bm, cm, dt_c, dt_r, a, dtb, reverse, yprev=None, dsk=None):
    b, n, _ = xs.shape
    lc = SSD_CHUNK
    nch = n // lc
    nctx = SSD_TILE // lc
    nbc = SSM_GROUPS * SSM_STATE
    if reverse:
        def cidx(j):
            return jnp.where(j < nctx, nctx - 1 - j, nch + nctx - 1 - j)
    else:
        def cidx(j):
            return j
    in_specs = [
        pl.BlockSpec((1, lc, D_INNER), lambda i, j: (i, cidx(j), 0)),
        pl.BlockSpec((1, lc, nbc), lambda i, j: (i, cidx(j), 0)),
        pl.BlockSpec((1, lc, nbc), lambda i, j: (i, cidx(j), 0)),
        pl.BlockSpec((1, lc, SSM_HEADS), lambda i, j: (i, cidx(j), 0)),
        pl.BlockSpec((1, SSM_HEADS, lc), lambda i, j: (i, 0, cidx(j))),
        pl.BlockSpec((1, SSM_HEADS), lambda i, j: (0, 0)),
        pl.BlockSpec((SSM_HEADS, 1), lambda i, j: (0, 0)),
        pl.BlockSpec((1, SSM_HEADS), lambda i, j: (0, 0)),
        pl.BlockSpec((SSM_HEADS, 1), lambda i, j: (0, 0)),
    ]
    args = [xs, bm, cm, dt_c, dt_r, a.reshape(1, -1), a.reshape(-1, 1), dtb.reshape(1, -1), dtb.reshape(-1, 1)]
    if reverse:
        in_specs += [
            pl.BlockSpec((1, lc, D_INNER), lambda i, j: (i, cidx(j), 0)),
            pl.BlockSpec((1, D_INNER), lambda i, j: (0, 0)),
        ]
        args += [yprev, dsk]
    return pl.pallas_call(
        functools.partial(_ssd_scan_kernel, reverse=reverse),
        grid=(b, nch),
        in_specs=in_specs,
        out_specs=pl.BlockSpec((1, lc, D_INNER), lambda i, j: (i, cidx(j), 0)),
        out_shape=jax.ShapeDtypeStruct((b, n, D_INNER), F32),
        scratch_shapes=[pltpu.VMEM((SSM_STATE, D_INNER), F32)],
        compiler_params=_cparams(2),
        name="ssd_scan_bwd" if reverse else "ssd_scan_fwd",
    )(*args)


def _ssd_out_kernel(y_ref, z_ref, ng_ref, w_ref, x_ref, mod_ref, lng_ref, lnb_ref, o_ref):
    y = y_ref[0] * _silu(z_ref[0])
    ms = jnp.mean(y * y, axis=-1, keepdims=True)
    yn = (y * lax.rsqrt(ms + RMS_EPS) * ng_ref[...]).astype(BF16)
    out = _dot(yn, w_ref[...])
    o_ref[0] = _ln_residual(x_ref[0], out, mod_ref[0, 0, 2:3, :], lng_ref[...], lnb_ref[...])


def _ssd_out(y, z, norm_g, w_out_bf, xcat, modcat, ln_g, ln_b):
    b, n, d = xcat.shape
    t = SSD_TILE
    return pl.pallas_call(
        _ssd_out_kernel,
        grid=(b, n // t),
        in_specs=[
            pl.BlockSpec((1, t, D_INNER), lambda i, j: (i, j, 0)),
            pl.BlockSpec((1, t, D_INNER), lambda i, j: (i, j, 0)),
            pl.BlockSpec((1, D_INNER), lambda i, j: (0, 0)),
            pl.BlockSpec((D_INNER, d), lambda i, j: (0, 0)),
            pl.BlockSpec((1, t, d), lambda i, j: (i, j, 0)),
            pl.BlockSpec((1, 1, 8, d), lambda i, j: (i, jnp.minimum(j, 1), 0, 0)),
            pl.BlockSpec((1, d), lambda i, j: (0, 0)),
            pl.BlockSpec((1, d), lambda i, j: (0, 0)),
        ],
        out_specs=pl.BlockSpec((1, t, d), lambda i, j: (i, j, 0)),
        out_shape=jax.ShapeDtypeStruct((b, n, d), F32),
        compiler_params=_cparams(2),
        name="ssd_out_ln",
    )(y, z, norm_g, w_out_bf, xcat, modcat, ln_g, ln_b)


def _router_kernel(x_ref, mod_ref, wrT_ref, h_ref, affT_ref):
    x = x_ref[0]
    h = x * (1.0 + mod_ref[0, 4:5, :]) + mod_ref[0, 3:4, :]
    h_ref[0] = h.astype(BF16)
    h1, h2, h3 = _split3(h)
    w1, w2, w3 = _split3(wrT_ref[...])
    lt = (_dot_nt(w1, h1) + _dot_nt(w1, h2) + _dot_nt(w2, h1)
          + _dot_nt(w2, h2) + _dot_nt(w1, h3) + _dot_nt(w3, h1))
    m = jnp.max(lt, axis=0, keepdims=True)
    e = jnp.exp(lt - m)
    affT_ref[0] = e / jnp.sum(e, axis=0, keepdims=True)


def _router(x, mod, wrT):
    b, n, d = x.shape
    t = min(ROW_TILE, n)
    return pl.pallas_call(
        _router_kernel,
        grid=(b, n // t),
        in_specs=[
            pl.BlockSpec((1, t, d), lambda i, j: (i, j, 0)),
            pl.BlockSpec((1, 8, d), lambda i, j: (i, 0, 0)),
            pl.BlockSpec((N_EXPERTS, d), lambda i, j: (0, 0)),
        ],
        out_specs=[
            pl.BlockSpec((1, t, d), lambda i, j: (i, j, 0)),
            pl.BlockSpec((1, N_EXPERTS, t), lambda i, j: (i, 0, j)),
        ],
        out_shape=[
            jax.ShapeDtypeStruct((b, n, d), BF16),
            jax.ShapeDtypeStruct((b, N_EXPERTS, n), F32),
        ],
        compiler_params=_cparams(2),
        name="moe_router",
    )(x, mod, wrT)


def _select_kernel(aff_ref, pos_ref, *, k):
    a = aff_ref[0]
    ne, r, _ = a.shape
    bits = pltpu.bitcast(a, I32)

    def count(mask):
        c = jnp.sum(jnp.where(mask, 1.0, 0.0), axis=1, keepdims=True)
        return jnp.sum(c, axis=2, keepdims=True)

    def body(i, thr):
        cand = thr | jnp.left_shift(jnp.int32(1), 30 - i)
        return jnp.where(count(bits >= cand) >= k, cand, thr)

    thr = lax.fori_loop(0, 31, body, jnp.zeros((ne, 1, 1), I32))
    gt = bits > thr
    eq = bits == thr
    need = k - count(gt)

    ci = lax.broadcasted_iota(I32, (LANES, LANES), 0)
    cj = lax.broadcasted_iota(I32, (LANES, LANES), 1)
    before = jnp.where(ci < cj, 1.0, 0.0).astype(BF16)
    ones = jnp.ones((LANES, LANES), BF16)
    ri = lax.broadcasted_iota(I32, (r, r), 0)
    rj = lax.broadcasted_iota(I32, (r, r), 1)
    rows_before = jnp.where(rj < ri, 1.0, 0.0).astype(BF16)

    def exclusive_prefix(flags):
        fb = flags.reshape(ne * r, LANES).astype(BF16)
        within = _dot(fb, before).reshape(ne, r, LANES)
        tot = _dot(fb, ones).reshape(ne, r, LANES).astype(BF16)
        rows = jnp.stack([_dot(rows_before, tot[e]) for e in range(ne)], axis=0)
        return within + rows

    eq_rank = exclusive_prefix(jnp.where(eq, 1.0, 0.0))
    sel = gt | (eq & (eq_rank < need))
    pos = exclusive_prefix(jnp.where(sel, 1.0, 0.0))
    pos_ref[0] = jnp.where(sel, pos, -1.0)


def _select(affT, k):
    b, ne, n = affT.shape
    if n < SEL_LEN:
        affT = jnp.pad(affT, ((0, 0), (0, 0), (0, SEL_LEN - n)), constant_values=-1.0)
    assert affT.shape[2] == SEL_LEN
    r = SEL_LEN // LANES
    pos = pl.pallas_call(
        functools.partial(_select_kernel, k=k),
        grid=(b,),
        in_specs=[pl.BlockSpec((1, ne, r, LANES), lambda i: (i, 0, 0, 0))],
        out_specs=pl.BlockSpec((1, ne, r, LANES), lambda i: (i, 0, 0, 0)),
        out_shape=jax.ShapeDtypeStruct((b, ne, r, LANES), F32),
        compiler_params=_cparams(1),
        name="moe_select",
    )(affT.reshape(b, ne, r, LANES))
    return pos.reshape(b, ne, SEL_LEN)[:, :, :n]


def _gather_kernel(lo_ref, nsub_ref, h_ref, pos_ref, xe_ref, acc_ref, *, win):
    t = pl.program_id(2)
    ntb = pl.num_programs(2)
    idx = (pl.program_id(0) * pl.num_programs(1) + pl.program_id(1)) * ntb + t
    cap = acc_ref.shape[0]

    @pl.when(t == 0)
    def _():
        acc_ref[...] = jnp.zeros(acc_ref.shape, F32)

    lo0 = lo_ref[idx]
    h = h_ref[0]
    pos = pos_ref[0, 0]

    def body(s, carry):
        lo = lo0 + s * win
        w0 = pl.multiple_of(jnp.minimum(lo, cap - win), SUBLANES)
        slot = w0 + lax.broadcasted_iota(I32, (win, 1), 0)
        slot = jnp.where(slot >= lo, slot, -2).astype(F32)
        onehot = jnp.where(pos == slot, 1.0, 0.0).astype(BF16)
        acc_ref[pl.ds(w0, win), :] += _dot(onehot, h)
        return carry

    lax.fori_loop(0, nsub_ref[idx], body, 0)

    @pl.when(t == ntb - 1)
    def _():
        xe_ref[0, 0] = acc_ref[...].astype(BF16)


def _gather(h_bf, pos, tables, cap, tb, win):
    b, n, d = h_bf.shape
    ne = pos.shape[1]
    ntb = n // tb
    lo, nsub = tables
    grid_spec = pltpu.PrefetchScalarGridSpec(
        num_scalar_prefetch=2,
        grid=(b, ne, ntb),
        in_specs=[
            pl.BlockSpec((1, tb, d), lambda i, e, t, a0, a1: (i, t, 0)),
            pl.BlockSpec((1, 1, 1, tb), lambda i, e, t, a0, a1: (i, e, 0, t)),
        ],
        out_specs=pl.BlockSpec((1, 1, cap, d), lambda i, e, t, a0, a1: (i, e, 0, 0)),
        scratch_shapes=[pltpu.VMEM((cap, d), F32)],
    )
    return pl.pallas_call(
        functools.partial(_gather_kernel, win=win),
        grid_spec=grid_spec,
        out_shape=jax.ShapeDtypeStruct((b, ne, cap, d), BF16),
        compiler_params=_cparams(3),
        name="moe_gather",
    )(lo, nsub, h_bf, pos.reshape(b, ne, 1, n))


def _ffn_kernel(xe_ref, wg_ref, wu_ref, wd_ref, ye_ref, acc_ref):
    f = pl.program_id(2)
    x = xe_ref[0, 0]
    gate = _dot(x, wg_ref[0, 0].astype(BF16))
    up = _dot(x, wu_ref[0, 0].astype(BF16))
    hid = (_silu(gate) * up).astype(BF16)
    contrib = _dot(hid, wd_ref[0, 0].astype(BF16))

    @pl.when(f == 0)
    def _():
        acc_ref[...] = contrib

    @pl.when(f > 0)
    def _():
        acc_ref[...] += contrib

    @pl.when(f == pl.num_programs(2) - 1)
    def _():
        ye_ref[0, 0] = acc_ref[...].astype(BF16)


def _expert_ffn(xe, w_gate, w_up, w_down, layer):
    b, ne, cap, d = xe.shape
    f = w_gate.shape[3]
    tf = FFN_TF
    return pl.pallas_call(
        _ffn_kernel,
        grid=(b, ne, f // tf),
        in_specs=[
            pl.BlockSpec((1, 1, cap, d), lambda i, e, j: (i, e, 0, 0)),
            pl.BlockSpec((1, 1, d, tf), lambda i, e, j: (layer, e, 0, j)),
            pl.BlockSpec((1, 1, d, tf), lambda i, e, j: (layer, e, 0, j)),
            pl.BlockSpec((1, 1, tf, d), lambda i, e, j: (layer, e, j, 0)),
        ],
        out_specs=pl.BlockSpec((1, 1, cap, d), lambda i, e, j: (i, e, 0, 0)),
        out_shape=jax.ShapeDtypeStruct((b, ne, cap, d), BF16),
        scratch_shapes=[pltpu.VMEM((cap, d), F32)],
        compiler_params=_cparams(3),
        name="moe_ffn",
    )(xe, w_gate, w_up, w_down)


def _scatter_kernel(lo_ref, nsub_ref, x_ref, mod_ref, pos_ref, aff_ref, ye_ref, lng_ref, lnb_ref, o_ref, acc_ref,
                    *, win):
    e = pl.program_id(2)
    ne = pl.num_programs(2)
    idx = (pl.program_id(0) * pl.num_programs(1) + pl.program_id(1)) * ne + e
    cap = ye_ref.shape[2]

    @pl.when(e == 0)
    def _():
        acc_ref[...] = jnp.zeros(acc_ref.shape, F32)

    lane = lax.broadcasted_iota(I32, (1, ne), 1)
    pos_e = jnp.sum(jnp.where(lane == e, pos_ref[0], 0.0), axis=1, keepdims=True)
    gate_e = jnp.sum(jnp.where(lane == e, aff_ref[0], 0.0), axis=1, keepdims=True)
    lo0 = lo_ref[idx]

    def body(s, carry):
        lo = lo0 + s * win
        w0 = pl.multiple_of(jnp.minimum(lo, cap - win), BF16_SUBLANES)
        slot = w0 + lax.broadcasted_iota(I32, (1, win), 1)
        slot = jnp.where(slot >= lo, slot, -2).astype(F32)
        onehot = jnp.where(pos_e == slot, 1.0, 0.0).astype(BF16)
        acc_ref[...] += gate_e * _dot(onehot, ye_ref[0, 0, pl.ds(w0, win), :])
        return carry

    lax.fori_loop(0, nsub_ref[idx], body, 0)

    @pl.when(e == ne - 1)
    def _():
        o_ref[0] = _ln_residual(x_ref[0], acc_ref[...], mod_ref[0, 5:6, :], lng_ref[...], lnb_ref[...])


def _scatter_ln(x, mod, pos_tok, aff_tok, ye, tables, ln_g, ln_b, tb, win):
    b, n, d = x.shape
    ne, cap = ye.shape[1], ye.shape[2]
    ntb = n // tb
    lo, nsub = tables
    grid_spec = pltpu.PrefetchScalarGridSpec(
        num_scalar_prefetch=2,
        grid=(b, ntb, ne),
        in_specs=[
            pl.BlockSpec((1, tb, d), lambda i, t, e, a0, a1: (i, t, 0)),
            pl.BlockSpec((1, 8, d), lambda i, t, e, a0, a1: (i, 0, 0)),
            pl.BlockSpec((1, tb, ne), lambda i, t, e, a0, a1: (i, t, 0)),
            pl.BlockSpec((1, tb, ne), lambda i, t, e, a0, a1: (i, t, 0)),
            pl.BlockSpec((1, 1, cap, d), lambda i, t, e, a0, a1: (i, e, 0, 0)),
            pl.BlockSpec((1, d), lambda i, t, e, a0, a1: (0, 0)),
            pl.BlockSpec((1, d), lambda i, t, e, a0, a1: (0, 0)),
        ],
        out_specs=pl.BlockSpec((1, tb, d), lambda i, t, e, a0, a1: (i, t, 0)),
        scratch_shapes=[pltpu.VMEM((tb, d), F32)],
    )
    return pl.pallas_call(
        functools.partial(_scatter_kernel, win=win),
        grid_spec=grid_spec,
        out_shape=jax.ShapeDtypeStruct((b, n, d), F32),
        compiler_params=_cparams(3),
        name="moe_scatter_ln",
    )(lo, nsub, x, mod, pos_tok, aff_tok, ye, ln_g, ln_b)


def _window_tables(pos, tb, win, align, token_major):
    b, ne, n = pos.shape
    ntb = n // tb
    cnt = jnp.sum((pos >= 0).reshape(b, ne, ntb, tb), axis=-1).astype(I32)
    start = jnp.cumsum(cnt, axis=-1) - cnt
    lo = (start // align) * align
    nsub = jnp.where(cnt > 0, (start - lo + cnt + win - 1) // win, 0)
    if token_major:
        lo, nsub = jnp.swapaxes(lo, 1, 2), jnp.swapaxes(nsub, 1, 2)
    return lo.reshape(-1).astype(I32), nsub.reshape(-1).astype(I32)


def _moe_layer(x, mod, wrT, w_gate, w_up, w_down, layer, ln_g, ln_b):
    b, n, d = x.shape
    cap = EC_CAPACITY_FACTOR * n // N_EXPERTS
    tb = min(MOE_TB, n)
    gwin = min(GATHER_WIN, cap)
    swin = min(SCATTER_WIN, cap)
    h_bf, affT = _router(x, mod, wrT)
    pos = _select(affT, cap)
    xe = _gather(h_bf, pos, _window_tables(pos, tb, gwin, SUBLANES, False), cap, tb, gwin)
    ye = _expert_ffn(xe, w_gate, w_up, w_down, layer)
    pos_tok = jnp.swapaxes(pos, 1, 2)
    aff_tok = jnp.swapaxes(affT, 1, 2)
    return _scatter_ln(x, mod, pos_tok, aff_tok, ye, _window_tables(pos, tb, swin, BF16_SUBLANES, True),
                       ln_g, ln_b, tb, swin)


def _rope_tables_T(n_tokens):
    rows = n_tokens // GRID_W
    row_idx = jnp.repeat(jnp.arange(rows, dtype=I32), GRID_W).astype(F32)
    col_idx = jnp.tile(jnp.arange(GRID_W, dtype=I32), rows).astype(F32)
    inv_freq = ROPE_THETA ** (-jnp.arange(ROPE_FREQS, dtype=F32) / ROPE_FREQS)
    ang = jnp.concatenate([inv_freq[:, None] * row_idx[None, :], inv_freq[:, None] * col_idx[None, :]], axis=0)
    return jnp.cos(ang), jnp.sin(ang)


def kernel(x, c, ctx, c_ctx, w_mod, b_mod, ln1_g, ln1_b, ln2_g, ln2_b, attn_w_qkv, attn_w_o, attn_q_g, attn_k_g, ssd_w_in, ssd_conv_w, ssd_conv_b, ssd_dt_bias, ssd_a_log, ssd_d, ssd_norm_g, ssd_w_out, pool_w, pool_scale, moe_router, moe_w_gate, moe_w_up, moe_w_down):
    b, n, d = x.shape
    lc = ctx.shape[1]
    assert d == D_MODEL and lc == SSD_TILE and b + 1 <= 8
    assert n % ROW_TILE == 0 and n % K_CHUNK == 0 and n <= SEL_LEN

    cond8 = jnp.concatenate([c, c_ctx[None, :], jnp.zeros((8 - b - 1, d), F32)], axis=0)
    mods = _modulation(cond8, w_mod, b_mod).reshape(DEPTH, 8, 6, d)
    mods = jnp.pad(mods, ((0, 0), (0, 0), (0, 2), (0, 0)))
    cosT, sinT = _rope_tables_T(n)
    cos_ctx = jnp.ones((ROPE_HALF, lc), F32)
    sin_ctx = jnp.zeros((ROPE_HALF, lc), F32)

    for i in range(DEPTH):
        last = i == DEPTH - 1
        kind, j = i % 3, i // 3
        mod_lat = mods[i, :b]
        mod_ctx = jnp.broadcast_to(mods[i, b][None], (b, 8, d))
        l1g, l1b = ln1_g[i][None, :], ln1_b[i][None, :]
        l2g, l2b = ln2_g[i][None, :], ln2_b[i][None, :]
        if kind == 0:
            w_bf = attn_w_qkv[j].astype(BF16)
            woT_bf = attn_w_o[j].T.astype(BF16)
            qg = attn_q_g[j][:, None]
            kg = attn_k_g[j][:, None]
            qT_l, k_l, vT_l = _qkv_project(x, mod_lat, w_bf, qg, kg, cosT, sinT)
            qT_c, k_c, vT_c = _qkv_project(ctx, mod_ctx, w_bf, qg, kg, cos_ctx, sin_ctx)
            bounded = _scores_bounded(attn_q_g[j], attn_k_g[j])
            oT_l = _flash_attention(bounded, qT_l, k_c, vT_c, k_l, vT_l)
            x = _attn_out(oT_l, woT_bf, x, mod_lat, l1g, l1b)
            if not last:
                oT_c = _flash_attention(bounded, qT_c, k_c, vT_c)
                ctx = _attn_out(oT_c, woT_bf, ctx, mod_ctx, l1g, l1b)
        elif kind == 1:
            xcat = jnp.concatenate([ctx, x], axis=1)
            modcat = jnp.stack([mod_ctx, mod_lat], axis=1)
            w_in = ssd_w_in[j]
            wz = w_in[:, :D_INNER].astype(BF16)
            wx = w_in[:, D_INNER:D_INNER + CONV_DIM].astype(BF16)
            wdt = jnp.pad(w_in[:, D_INNER + CONV_DIM:], ((0, 0), (0, LANES - 2 * SSM_HEADS))).astype(BF16)
            z, xbc, dt_raw = _ssd_in_proj(xcat, modcat, wz, wx, wdt)
            conv_w8 = jnp.pad(ssd_conv_w[j], ((0, 8 - CONV_WIDTH), (0, 0)))
            xs, bm, cm = _ssd_conv(xbc, conv_w8, ssd_conv_b[j][None, :])
            a = -jnp.exp(ssd_a_log[j].astype(F32))
            dtb = ssd_dt_bias[j].astype(F32)
            dsk = jnp.repeat(ssd_d[j], SSM_HEADDIM)[None, :]
            dt_f = dt_raw[:, :, :SSM_HEADS]
            dt_b = dt_raw[:, :, SSM_HEADS:2 * SSM_HEADS]
            y_f = _ssd_scan(xs, bm, cm, dt_f, jnp.swapaxes(dt_f, 1, 2), a[0], dtb[0], False)
            y = _ssd_scan(xs, bm, cm, dt_b, jnp.swapaxes(dt_b, 1, 2), a[1], dtb[1], True, y_f, dsk)
            xcat = _ssd_out(y, z, ssd_norm_g[j][None, :], ssd_w_out[j].astype(BF16), xcat, modcat, l1g, l1b)
            ctx, x = xcat[:, :lc], xcat[:, lc:]
        else:
            pw = pool_w[j].astype(BF16)
            ps = pool_scale[j][None, :]
            x = _pool_mixer(x, mod_lat, pw, ps, l1g, l1b)
            if not last:
                ctx = _pool_mixer(ctx, mod_ctx, pw, ps, l1g, l1b)
        wrT = moe_router[i].T
        x = _moe_layer(x, mod_lat, wrT, moe_w_gate, moe_w_up, moe_w_down, i, l2g, l2b)
        if not last:
            ctx = _moe_layer(ctx, mod_ctx, wrT, moe_w_gate, moe_w_up, moe_w_down, i, l2g, l2b)
    return x
```

```python
import functools
import math

import jax
import jax.numpy as jnp
from jax import lax
from jax.experimental import pallas as pl
from jax.experimental.pallas import tpu as pltpu

F32 = jnp.float32
BF16 = jnp.bfloat16
I32 = jnp.int32

D_MODEL = 1024
DEPTH = 4
GRID_W = 64
N_HEADS = 16
N_KV_HEADS = 4
GQA_GROUP = N_HEADS // N_KV_HEADS
HEAD_DIM = 64
ROPE_HALF = HEAD_DIM // 2
ROPE_FREQS = HEAD_DIM // 4
ROPE_THETA = 10000.0
D_INNER = 2048
SSM_HEADDIM = 64
SSM_HEADS = 32
SSM_GROUPS = 4
SSM_STATE = 128
CONV_WIDTH = 5
CONV_DIM = D_INNER + 2 * SSM_GROUPS * SSM_STATE
SSD_CHUNK = 128
POOL_WINDOWS = (2, 4, 8, 16)
POOL_GROUP_DIM = D_MODEL // 4
N_EXPERTS = 16
D_EXPERT = 2048
EC_CAPACITY_FACTOR = 2
DEEPNORM_ALPHA = (2.0 * DEPTH) ** 0.25
LN_EPS = 1e-5
RMS_EPS = 1e-6

LANES = 128
SUBLANES = 8
VMEM_LIMIT = 56 * 1024 * 1024

ROW_TILE = 512
SSD_TILE = 256
Q_TILE = 256
K_CHUNK = 512
PAIR_UNROLL = 3
MOE_TB = 1024
GATHER_WIN = 128
SCATTER_WIN = 256
BF16_SUBLANES = 16
FFN_TF = 512
FFN_ROWS = 512
SEL_LEN = 16384
HALO = 8
Q_SCALE = (HEAD_DIM ** -0.5) * math.log2(math.e)
NEG_BIG = -1e30
MXU_COLS = 256
SCORE_BOUND = 60.0
BF16_SLACK = 1.01


def _cparams(n_axes):
    return pltpu.CompilerParams(dimension_semantics=("arbitrary",) * n_axes, vmem_limit_bytes=VMEM_LIMIT)


def _dot(a, b):
    return jnp.dot(a, b, preferred_element_type=F32)


def _dot_nt(a, b):
    return lax.dot_general(a, b, (((1,), (1,)), ((), ())), preferred_element_type=F32)


def _split2(a):
    hi = a.astype(BF16)
    lo = (a - hi.astype(F32)).astype(BF16)
    return hi, lo


def _split3(a):
    hi = a.astype(BF16)
    r = a - hi.astype(F32)
    mid = r.astype(BF16)
    lo = (r - mid.astype(F32)).astype(BF16)
    return hi, mid, lo


def _silu(v):
    return v * (1.0 / (1.0 + jnp.exp(-v)))


def _ln_residual(x, y, gate, ln_g, ln_b):
    v = DEEPNORM_ALPHA * x + gate * y
    mu = jnp.mean(v, axis=-1, keepdims=True)
    d = v - mu
    var = jnp.mean(d * d, axis=-1, keepdims=True)
    return d * lax.rsqrt(var + LN_EPS) * ln_g + ln_b


def _mod_kernel(c_ref, w_ref, b_ref, o_ref):
    s = _silu(c_ref[...])
    s_hi, s_lo = _split2(s)
    w = w_ref[0]
    w_hi, w_lo = _split2(w)
    o_ref[0] = _dot(s_hi, w_hi) + _dot(s_lo, w_hi) + _dot(s_hi, w_lo) + b_ref[0]


def _modulation(cond8, w_mod, b_mod):
    depth, d, d6 = w_mod.shape
    tn = 1536
    return pl.pallas_call(
        _mod_kernel,
        grid=(depth, d6 // tn),
        in_specs=[
            pl.BlockSpec((8, d), lambda i, j: (0, 0)),
            pl.BlockSpec((1, d, tn), lambda i, j: (i, 0, j)),
            pl.BlockSpec((1, 1, tn), lambda i, j: (i, 0, j)),
        ],
        out_specs=pl.BlockSpec((1, 8, tn), lambda i, j: (i, 0, j)),
        out_shape=jax.ShapeDtypeStruct((depth, 8, d6), F32),
        compiler_params=_cparams(2),
        name="modulation",
    )(cond8, w_mod, b_mod.reshape(depth, 1, d6))


def _qkv_kernel(x_ref, mod_ref, w_ref, qg_ref, kg_ref, cos_ref, sin_ref, qT_ref, k_ref, vT_ref):
    t = x_ref.shape[1]
    x = x_ref[0]
    h = (x * (1.0 + mod_ref[0, 1:2, :]) + mod_ref[0, 0:1, :]).astype(BF16)
    acc = _dot(h, w_ref[...])
    accT = acc.T
    cos = cos_ref[...][None]
    sin = sin_ref[...][None]

    def norm_rope(tT, n_heads, g):
        t3 = tT.reshape(n_heads, HEAD_DIM, t)
        ms = jnp.mean(t3 * t3, axis=1, keepdims=True)
        t3 = t3 * lax.rsqrt(ms + RMS_EPS) * g[None]
        t1 = t3[:, :ROPE_HALF, :]
        t2 = t3[:, ROPE_HALF:, :]
        return jnp.concatenate([t1 * cos - t2 * sin, t2 * cos + t1 * sin], axis=1)

    nq = N_HEADS * HEAD_DIM
    nk = N_KV_HEADS * HEAD_DIM
    q3 = norm_rope(accT[:nq], N_HEADS, qg_ref[...]) * Q_SCALE
    qT_ref[0] = q3.reshape(nq, t).astype(BF16)
    k3 = norm_rope(accT[nq:nq + nk], N_KV_HEADS, kg_ref[...])
    k_ref[0] = k3.reshape(nk, t).T.astype(BF16)
    vT_ref[0, 0] = accT[nq + nk:].astype(BF16)


def _qkv_project(x, mod, w_bf, qg, kg, cosT, sinT):
    b, n, d = x.shape
    t = min(ROW_TILE, n)
    nq = N_HEADS * HEAD_DIM
    nk = N_KV_HEADS * HEAD_DIM
    return pl.pallas_call(
        _qkv_kernel,
        grid=(b, n // t),
        in_specs=[
            pl.BlockSpec((1, t, d), lambda i, j: (i, j, 0)),
            pl.BlockSpec((1, 8, d), lambda i, j: (i, 0, 0)),
            pl.BlockSpec((d, nq + 2 * nk), lambda i, j: (0, 0)),
            pl.BlockSpec((HEAD_DIM, 1), lambda i, j: (0, 0)),
            pl.BlockSpec((HEAD_DIM, 1), lambda i, j: (0, 0)),
            pl.BlockSpec((ROPE_HALF, t), lambda i, j: (0, j)),
            pl.BlockSpec((ROPE_HALF, t), lambda i, j: (0, j)),
        ],
        out_specs=[
            pl.BlockSpec((1, nq, t), lambda i, j: (i, 0, j)),
            pl.BlockSpec((1, t, nk), lambda i, j: (i, j, 0)),
            pl.BlockSpec((1, 1, nk, t), lambda i, j: (i, j, 0, 0)),
        ],
        out_shape=[
            jax.ShapeDtypeStruct((b, nq, n), BF16),
            jax.ShapeDtypeStruct((b, n, nk), BF16),
            jax.ShapeDtypeStruct((b, n // t, nk, t), BF16),
        ],
        compiler_params=_cparams(2),
        name="qkv_project",
    )(x, mod, w_bf, qg, kg, cosT, sinT)


def _flash_kernel(*refs, n_lat_chunks, tk_lat):
    if n_lat_chunks:
        bounded_ref, qT_ref, kc_ref, vTc_ref, kl_ref, vTl_ref, o_ref, acc_ref, m_ref, sa_ref, sb_ref = refs
    else:
        bounded_ref, qT_ref, kc_ref, vTc_ref, o_ref, acc_ref, m_ref, sa_ref, sb_ref = refs
    g = pl.program_id(1)
    tq = qT_ref.shape[2]
    qb = qT_ref[0]
    q64 = jnp.concatenate([qb[j * HEAD_DIM:(j + 1) * HEAD_DIM, :] for j in range(GQA_GROUP)], axis=1)
    q256 = jnp.concatenate([q64] * N_KV_HEADS, axis=0).astype(F32)
    row_group = lax.broadcasted_iota(I32, (N_KV_HEADS * HEAD_DIM, 1), 0) // HEAD_DIM
    qpad = jnp.where(row_group == g, q256, 0.0).astype(BF16)
    col_tiles = [slice(c * MXU_COLS, (c + 1) * MXU_COLS) for c in range(GQA_GROUP * tq // MXU_COLS)]

    acc_ref[...] = jnp.zeros(acc_ref.shape, F32)

    def with_ones(vT_blk):
        return jnp.concatenate([vT_blk, jnp.ones((16, vT_blk.shape[1]), BF16)], axis=0)

    def produce(s_ref, k_blk):
        for cs in col_tiles:
            s_ref[0:k_blk.shape[0], cs] = _dot(k_blk, qpad[:, cs])

    def consume(s_ref, vT_blk):
        va = with_ones(vT_blk)
        for cs in col_tiles:
            p = jnp.exp2(s_ref[0:vT_blk.shape[1], cs]).astype(BF16)
            acc_ref[:, cs] += _dot(va, p)

    def lat_k(i):
        return kl_ref[0, pl.ds(pl.multiple_of(i * tk_lat, tk_lat), tk_lat), :]

    def run_bounded():
        produce(sa_ref, kc_ref[0])
        if n_lat_chunks == 0:
            consume(sa_ref, vTc_ref[0, 0])
            return
        produce(sb_ref, lat_k(0))
        consume(sa_ref, vTc_ref[0, 0])

        def pair(jj, carry):
            i = 2 * jj
            produce(sa_ref, lat_k(i + 1))
            consume(sb_ref, vTl_ref[0, i])
            produce(sb_ref, lat_k(i + 2))
            consume(sa_ref, vTl_ref[0, i + 1])
            return carry

        lax.fori_loop(0, (n_lat_chunks - 1) // 2, pair, 0, unroll=PAIR_UNROLL)
        if (n_lat_chunks - 1) % 2:
            produce(sa_ref, lat_k(n_lat_chunks - 1))
            consume(sb_ref, vTl_ref[0, n_lat_chunks - 2])
            consume(sa_ref, vTl_ref[0, n_lat_chunks - 1])
        else:
            consume(sb_ref, vTl_ref[0, n_lat_chunks - 1])

    def online_step(k_blk, vT_blk):
        va = with_ones(vT_blk)
        for cs in col_tiles:
            s = _dot(k_blk, qpad[:, cs])
            m_old = m_ref[:, cs]
            m_new = jnp.maximum(m_old, jnp.max(s, axis=0, keepdims=True))
            p = jnp.exp2(s - m_new).astype(BF16)
            acc_ref[:, cs] = acc_ref[:, cs] * jnp.exp2(m_old - m_new) + _dot(va, p)
            m_ref[:, cs] = m_new

    def run_online():
        m_ref[...] = jnp.full(m_ref.shape, -jnp.inf, F32)
        online_step(kc_ref[0], vTc_ref[0, 0])
        if n_lat_chunks:
            def body(i, carry):
                online_step(lat_k(i), vTl_ref[0, i])
                return carry
            lax.fori_loop(0, n_lat_chunks, body, 0)

    pl.when(bounded_ref[0] == 1)(run_bounded)
    pl.when(bounded_ref[0] == 0)(run_online)

    acc = acc_ref[...]
    o = acc[:HEAD_DIM] * (1.0 / acc[HEAD_DIM:HEAD_DIM + 1])
    for j in range(GQA_GROUP):
        o_ref[0, j * HEAD_DIM:(j + 1) * HEAD_DIM, :] = o[:, j * tq:(j + 1) * tq].astype(BF16)


def _flash_attention(bounded, qT, k_ctx, vT_ctx, k_lat=None, vT_lat=None):
    b, nq, n = qT.shape
    tq = min(Q_TILE, n)
    lc = k_ctx.shape[1]
    nk = N_KV_HEADS * HEAD_DIM
    gw = GQA_GROUP * HEAD_DIM
    in_specs = [
        pl.BlockSpec((1, gw, tq), lambda i, g, j, fl: (i, g, j)),
        pl.BlockSpec((1, lc, nk), lambda i, g, j, fl: (i, 0, 0)),
        pl.BlockSpec((1, 1, HEAD_DIM, lc), lambda i, g, j, fl: (i, 0, g, 0)),
    ]
    args = [qT, k_ctx, vT_ctx]
    n_chunks, tk = 0, 0
    if k_lat is not None:
        n_chunks, tk = vT_lat.shape[1], vT_lat.shape[3]
        in_specs += [
            pl.BlockSpec((1, k_lat.shape[1], nk), lambda i, g, j, fl: (i, 0, 0)),
            pl.BlockSpec((1, n_chunks, HEAD_DIM, tk), lambda i, g, j, fl: (i, 0, g, 0)),
        ]
        args += [k_lat, vT_lat]
    grid_spec = pltpu.PrefetchScalarGridSpec(
        num_scalar_prefetch=1,
        grid=(b, N_KV_HEADS, n // tq),
        in_specs=in_specs,
        out_specs=pl.BlockSpec((1, gw, tq), lambda i, g, j, fl: (i, g, j)),
        scratch_shapes=[
            pltpu.VMEM((HEAD_DIM + 16, GQA_GROUP * tq), F32),
            pltpu.VMEM((1, GQA_GROUP * tq), F32),
            pltpu.VMEM((max(tk, lc), GQA_GROUP * tq), F32),
            pltpu.VMEM((max(tk, lc), GQA_GROUP * tq), F32),
        ],
    )
    return pl.pallas_call(
        functools.partial(_flash_kernel, n_lat_chunks=n_chunks, tk_lat=tk),
        grid_spec=grid_spec,
        out_shape=jax.ShapeDtypeStruct((b, nq, n), BF16),
        compiler_params=_cparams(3),
        name="flash_attention",
    )(bounded, *args)


def _scores_bounded(q_g, k_g):
    bound = HEAD_DIM * jnp.max(jnp.abs(q_g)) * jnp.max(jnp.abs(k_g)) * (Q_SCALE * BF16_SLACK)
    return (bound <= SCORE_BOUND).astype(I32).reshape(1)


def _attn_out_kernel(oT_ref, woT_ref, x_ref, mod_ref, lng_ref, lnb_ref, o_ref):
    yT = _dot(woT_ref[...], oT_ref[0])
    o_ref[0] = _ln_residual(x_ref[0], yT.T, mod_ref[0, 2:3, :], lng_ref[...], lnb_ref[...])


def _attn_out(oT, woT_bf, x, mod, ln_g, ln_b):
    b, n, d = x.shape
    t = min(ROW_TILE, n)
    return pl.pallas_call(
        _attn_out_kernel,
        grid=(b, n // t),
        in_specs=[
            pl.BlockSpec((1, d, t), lambda i, j: (i, 0, j)),
            pl.BlockSpec((d, d), lambda i, j: (0, 0)),
            pl.BlockSpec((1, t, d), lambda i, j: (i, j, 0)),
            pl.BlockSpec((1, 8, d), lambda i, j: (i, 0, 0)),
            pl.BlockSpec((1, d), lambda i, j: (0, 0)),
            pl.BlockSpec((1, d), lambda i, j: (0, 0)),
        ],
        out_specs=pl.BlockSpec((1, t, d), lambda i, j: (i, j, 0)),
        out_shape=jax.ShapeDtypeStruct((b, n, d), F32),
        compiler_params=_cparams(2),
        name="attn_out_ln",
    )(oT, woT_bf, x, mod, ln_g, ln_b)


def _pool_kernel(x_ref, xp_ref, xn_ref, mod_ref, w_ref, ps_ref, lng_ref, lnb_ref, o_ref, scr_ref, *, n_tokens):
    j = pl.program_id(1)
    nt = pl.num_programs(1)
    t = x_ref.shape[1]
    sh = mod_ref[0, 0:1, :]
    sc = mod_ref[0, 1:2, :]
    x = x_ref[0]
    h = x * (1.0 + sc) + sh
    hp = xp_ref[0] * (1.0 + sc) + sh
    hn = xn_ref[0] * (1.0 + sc) + sh
    scr_ref[0:HALO, :] = jnp.where(j > 0, hp, 0.0)
    scr_ref[HALO:HALO + t, :] = h
    scr_ref[HALO + t:HALO + t + HALO, :] = jnp.where(j < nt - 1, hn, 0.0)
    tok = lax.broadcasted_iota(I32, (t, 1), 0) + j * t
    ys = []
    for gi, win in enumerate(POOL_WINDOWS):
        cols = slice(gi * POOL_GROUP_DIM, (gi + 1) * POOL_GROUP_DIM)
        half = win // 2
        acc = scr_ref[HALO - half:HALO - half + t, cols]
        for off in range(-half + 1, win - half):
            acc = acc + scr_ref[HALO + off:HALO + off + t, cols]
        lo = jnp.maximum(tok - half, 0)
        hi = jnp.minimum(tok + (win - half), n_tokens)
        cnt = (hi - lo).astype(F32)
        pooled = acc / cnt - h[:, cols]
        ys.append(_dot(pooled.astype(BF16), w_ref[gi]))
    y = jnp.concatenate(ys, axis=1) * ps_ref[...]
    o_ref[0] = _ln_residual(x, y, mod_ref[0, 2:3, :], lng_ref[...], lnb_ref[...])


def _pool_mixer(x, mod, w_bf, pool_scale, ln_g, ln_b):
    b, n, d = x.shape
    t = min(SSD_TILE, n)
    hb = t // HALO
    nb = n // HALO
    return pl.pallas_call(
        functools.partial(_pool_kernel, n_tokens=n),
        grid=(b, n // t),
        in_specs=[
            pl.BlockSpec((1, t, d), lambda i, j: (i, j, 0)),
            pl.BlockSpec((1, HALO, d), lambda i, j: (i, jnp.maximum(j * hb - 1, 0), 0)),
            pl.BlockSpec((1, HALO, d), lambda i, j: (i, jnp.minimum((j + 1) * hb, nb - 1), 0)),
            pl.BlockSpec((1, 8, d), lambda i, j: (i, 0, 0)),
            pl.BlockSpec((4, POOL_GROUP_DIM, POOL_GROUP_DIM), lambda i, j: (0, 0, 0)),
            pl.BlockSpec((1, d), lambda i, j: (0, 0)),
            pl.BlockSpec((1, d), lambda i, j: (0, 0)),
            pl.BlockSpec((1, d), lambda i, j: (0, 0)),
        ],
        out_specs=pl.BlockSpec((1, t, d), lambda i, j: (i, j, 0)),
        out_shape=jax.ShapeDtypeStruct((b, n, d), F32),
        scratch_shapes=[pltpu.VMEM((t + 2 * HALO, d), F32)],
        compiler_params=_cparams(2),
        name="pool_mixer_ln",
    )(x, x, x, mod, w_bf, pool_scale, ln_g, ln_b)


def _ssd_in_kernel(x_ref, mod_ref, wz_ref, wx_ref, wdt_ref, z_ref, xbc_ref, dt_ref):
    x = x_ref[0]
    h = (x * (1.0 + mod_ref[0, 0, 1:2, :]) + mod_ref[0, 0, 0:1, :]).astype(BF16)
    z_ref[0] = _dot(h, wz_ref[...])
    xbc_ref[0] = _dot(h, wx_ref[...])
    dt_ref[0] = _dot(h, wdt_ref[...])


def _ssd_in_proj(xcat, modcat, wz, wx, wdt):
    b, n, d = xcat.shape
    t = SSD_TILE
    return pl.pallas_call(
        _ssd_in_kernel,
        grid=(b, n // t),
        in_specs=[
            pl.BlockSpec((1, t, d), lambda i, j: (i, j, 0)),
            pl.BlockSpec((1, 1, 8, d), lambda i, j: (i, jnp.minimum(j, 1), 0, 0)),
            pl.BlockSpec((d, D_INNER), lambda i, j: (0, 0)),
            pl.BlockSpec((d, CONV_DIM), lambda i, j: (0, 0)),
            pl.BlockSpec((d, LANES), lambda i, j: (0, 0)),
        ],
        out_specs=[
            pl.BlockSpec((1, t, D_INNER), lambda i, j: (i, j, 0)),
            pl.BlockSpec((1, t, CONV_DIM), lambda i, j: (i, j, 0)),
            pl.BlockSpec((1, t, LANES), lambda i, j: (i, j, 0)),
        ],
        out_shape=[
            jax.ShapeDtypeStruct((b, n, D_INNER), F32),
            jax.ShapeDtypeStruct((b, n, CONV_DIM), F32),
            jax.ShapeDtypeStruct((b, n, LANES), F32),
        ],
        compiler_params=_cparams(2),
        name="ssd_in_proj",
    )(xcat, modcat, wz, wx, wdt)


def _ssd_conv_kernel(x_ref, xp_ref, xn_ref, w_ref, b_ref, xs_ref, bm_ref, cm_ref, scr_ref):
    j = pl.program_id(1)
    nt = pl.num_programs(1)
    t = x_ref.shape[1]
    scr_ref[0:HALO, :] = jnp.where(j > 1, xp_ref[0], 0.0)
    scr_ref[HALO:HALO + t, :] = x_ref[0]
    scr_ref[HALO + t:HALO + t + HALO, :] = jnp.where((j > 0) & (j < nt - 1), xn_ref[0], 0.0)
    pad = CONV_WIDTH // 2
    acc = b_ref[...] + w_ref[0:1, :] * scr_ref[HALO - pad:HALO - pad + t, :]
    for kk in range(1, CONV_WIDTH):
        acc = acc + w_ref[kk:kk + 1, :] * scr_ref[HALO - pad + kk:HALO - pad + kk + t, :]
    y = _silu(acc)
    nbc = SSM_GROUPS * SSM_STATE
    xs_ref[0] = y[:, :D_INNER]
    bm_ref[0] = y[:, D_INNER:D_INNER + nbc]
    cm_ref[0] = y[:, D_INNER + nbc:]


def _ssd_conv(xbc, conv_w8, conv_b):
    b, n, cdim = xbc.shape
    t = SSD_TILE
    hb = t // HALO
    nb = n // HALO
    nbc = SSM_GROUPS * SSM_STATE
    return pl.pallas_call(
        _ssd_conv_kernel,
        grid=(b, n // t),
        in_specs=[
            pl.BlockSpec((1, t, cdim), lambda i, j: (i, j, 0)),
            pl.BlockSpec((1, HALO, cdim), lambda i, j: (i, jnp.maximum(j * hb - 1, 0), 0)),
            pl.BlockSpec((1, HALO, cdim), lambda i, j: (i, jnp.minimum((j + 1) * hb, nb - 1), 0)),
            pl.BlockSpec((8, cdim), lambda i, j: (0, 0)),
            pl.BlockSpec((1, cdim), lambda i, j: (0, 0)),
        ],
        out_specs=[
            pl.BlockSpec((1, t, D_INNER), lambda i, j: (i, j, 0)),
            pl.BlockSpec((1, t, nbc), lambda i, j: (i, j, 0)),
            pl.BlockSpec((1, t, nbc), lambda i, j: (i, j, 0)),
        ],
        out_shape=[
            jax.ShapeDtypeStruct((b, n, D_INNER), F32),
            jax.ShapeDtypeStruct((b, n, nbc), F32),
            jax.ShapeDtypeStruct((b, n, nbc), F32),
        ],
        scratch_shapes=[pltpu.VMEM((t + 2 * HALO, cdim), F32)],
        compiler_params=_cparams(2),
        name="ssd_conv",
    )(xbc, xbc, xbc, conv_w8, conv_b)


def _softplus(v):
    return jnp.maximum(v, 0.0) + jnp.log(1.0 + jnp.exp(-jnp.abs(v)))


def _ssd_scan_kernel(*refs, reverse):
    if reverse:
        (xs_ref, bm_ref, cm_ref, dtc_ref, dtr_ref, arow_ref, acol_ref, brow_ref, bcol_ref,
         yprev_ref, dsk_ref, y_ref, h_ref) = refs
    else:
        (xs_ref, bm_ref, cm_ref, dtc_ref, dtr_ref, arow_ref, acol_ref, brow_ref, bcol_ref,
         y_ref, h_ref) = refs
    lc = SSD_CHUNK
    hg = SSM_HEADS // SSM_GROUPS
    gw = hg * SSM_HEADDIM

    @pl.when(pl.program_id(1) == 0)
    def _():
        h_ref[...] = jnp.zeros(h_ref.shape, F32)

    xs = xs_ref[0]
    bm = bm_ref[0]
    cm = cm_ref[0]
    dt_c = _softplus(dtc_ref[0] + brow_ref[...])
    dt_r = _softplus(dtr_ref[0] + bcol_ref[...])
    da_c = dt_c * arow_ref[...]
    da_r = dt_r * acol_ref[...]

    li = lax.broadcasted_iota(I32, (lc, lc), 0)
    si = lax.broadcasted_iota(I32, (lc, lc), 1)
    keep = (si >= li) if reverse else (si <= li)
    tri = jnp.where(keep, 1.0, 0.0).astype(BF16)
    triT = jnp.where((li >= si) if reverse else (li <= si), 1.0, 0.0).astype(BF16)
    c1, c2, c3 = _split3(da_c)
    cum_c = _dot(tri, c1) + _dot(tri, c2) + _dot(tri, c3)
    r1, r2, r3 = _split3(da_r)
    cum_r = _dot(r1, triT) + _dot(r2, triT) + _dot(r3, triT)
    end = 0 if reverse else lc - 1
    cum_end = cum_c[end:end + 1, :]

    hrow = lax.broadcasted_iota(I32, (SSM_HEADS, D_INNER), 0)
    hcol = lax.broadcasted_iota(I32, (SSM_HEADS, D_INNER), 1) // SSM_HEADDIM
    expand = jnp.where(hrow == hcol, 1.0, 0.0).astype(BF16)

    def expand_heads(v):
        v1, v2 = _split2(v)
        return _dot(v1, expand) + _dot(v2, expand)

    e_in = expand_heads(jnp.exp(cum_c))
    w_end = expand_heads(jnp.exp(cum_end - cum_c) * dt_c)
    cd = e_in[end:end + 1, :]

    lane = lax.broadcasted_iota(I32, (lc, LANES), 1)
    xs_bf = xs.astype(BF16)
    y_parts = []
    for g in range(SSM_GROUPS):
        cm_g = cm[:, g * SSM_STATE:(g + 1) * SSM_STATE].astype(BF16)
        bm_g = bm[:, g * SSM_STATE:(g + 1) * SSM_STATE]
        cb = _dot_nt(cm_g, bm_g.astype(BF16))
        h_g = h_ref[:, g * gw:(g + 1) * gw]
        y_off = _dot(cm_g, h_g.astype(BF16)) * e_in[:, g * gw:(g + 1) * gw]
        pair_out = []
        for pr in range(hg // 2):
            xs_pair = xs_bf[:, g * gw + pr * LANES:g * gw + (pr + 1) * LANES]
            outs = []
            for hh in (g * hg + 2 * pr, g * hg + 2 * pr + 1):
                diff = cum_c[:, hh:hh + 1] - cum_r[hh:hh + 1, :]
                dec = jnp.exp(jnp.where(keep, diff, NEG_BIG))
                mix = cb * dec * dt_r[hh:hh + 1, :]
                outs.append(_dot(mix.astype(BF16), xs_pair))
            pair_out.append(jnp.where(lane < SSM_HEADDIM, outs[0], outs[1]))
        y_parts.append(jnp.concatenate(pair_out, axis=1) + y_off)
        wx = (w_end[:, g * gw:(g + 1) * gw] * xs[:, g * gw:(g + 1) * gw]).astype(BF16)
        h_ref[:, g * gw:(g + 1) * gw] = h_g * cd[:, g * gw:(g + 1) * gw] + _dot(bm_g.T.astype(BF16), wx)
    y = jnp.concatenate(y_parts, axis=1)
    if reverse:
        y = y + yprev_ref[0] + dsk_ref[...] * xs
    y_ref[0] = y


def _ssd_scan(xs, bm, cm, dt_c, dt_r, a, dtb, reverse, yprev=None, dsk=None):
    b, n, _ = xs.shape
    lc = SSD_CHUNK
    nch = n // lc
    nctx = SSD_TILE // lc
    nbc = SSM_GROUPS * SSM_STATE
    if reverse:
        def cidx(j):
            return jnp.where(j < nctx, nctx - 1 - j, nch + nctx - 1 - j)
    else:
        def cidx(j):
            return j
    in_specs = [
        pl.BlockSpec((1, lc, D_INNER), lambda i, j: (i, cidx(j), 0)),
        pl.BlockSpec((1, lc, nbc), lambda i, j: (i, cidx(j), 0)),
        pl.BlockSpec((1, lc, nbc), lambda i, j: (i, cidx(j), 0)),
        pl.BlockSpec((1, lc, SSM_HEADS), lambda i, j: (i, cidx(j), 0)),
        pl.BlockSpec((1, SSM_HEADS, lc), lambda i, j: (i, 0, cidx(j))),
        pl.BlockSpec((1, SSM_HEADS), lambda i, j: (0, 0)),
        pl.BlockSpec((SSM_HEADS, 1), lambda i, j: (0, 0)),
        pl.BlockSpec((1, SSM_HEADS), lambda i, j: (0, 0)),
        pl.BlockSpec((SSM_HEADS, 1), lambda i, j: (0, 0)),
    ]
    args = [xs, bm, cm, dt_c, dt_r, a.reshape(1, -1), a.reshape(-1, 1), dtb.reshape(1, -1), dtb.reshape(-1, 1)]
    if reverse:
        in_specs += [
            pl.BlockSpec((1, lc, D_INNER), lambda i, j: (i, cidx(j), 0)),
            pl.BlockSpec((1, D_INNER), lambda i, j: (0, 0)),
        ]
        args += [yprev, dsk]
    return pl.pallas_call(
        functools.partial(_ssd_scan_kernel, reverse=reverse),
        grid=(b, nch),
        in_specs=in_specs,
        out_specs=pl.BlockSpec((1, lc, D_INNER), lambda i, j: (i, cidx(j), 0)),
        out_shape=jax.ShapeDtypeStruct((b, n, D_INNER), F32),
        scratch_shapes=[pltpu.VMEM((SSM_STATE, D_INNER), F32)],
        compiler_params=_cparams(2),
        name="ssd_scan_bwd" if reverse else "ssd_scan_fwd",
    )(*args)


def _ssd_out_kernel(y_ref, z_ref, ng_ref, w_ref, x_ref, mod_ref, lng_ref, lnb_ref, o_ref):
    y = y_ref[0] * _silu(z_ref[0])
    ms = jnp.mean(y * y, axis=-1, keepdims=True)
    yn = (y * lax.rsqrt(ms + RMS_EPS) * ng_ref[...]).astype(BF16)
    out = _dot(yn, w_ref[...])
    o_ref[0] = _ln_residual(x_ref[0], out, mod_ref[0, 0, 2:3, :], lng_ref[...], lnb_ref[...])


def _ssd_out(y, z, norm_g, w_out_bf, xcat, modcat, ln_g, ln_b):
    b, n, d = xcat.shape
    t = SSD_TILE
    return pl.pallas_call(
        _ssd_out_kernel,
        grid=(b, n // t),
        in_specs=[
            pl.BlockSpec((1, t, D_INNER), lambda i, j: (i, j, 0)),
            pl.BlockSpec((1, t, D_INNER), lambda i, j: (i, j, 0)),
            pl.BlockSpec((1, D_INNER), lambda i, j: (0, 0)),
            pl.BlockSpec((D_INNER, d), lambda i, j: (0, 0)),
            pl.BlockSpec((1, t, d), lambda i, j: (i, j, 0)),
            pl.BlockSpec((1, 1, 8, d), lambda i, j: (i, jnp.minimum(j, 1), 0, 0)),
            pl.BlockSpec((1, d), lambda i, j: (0, 0)),
            pl.BlockSpec((1, d), lambda i, j: (0, 0)),
        ],
        out_specs=pl.BlockSpec((1, t, d), lambda i, j: (i, j, 0)),
        out_shape=jax.ShapeDtypeStruct((b, n, d), F32),
        compiler_params=_cparams(2),
        name="ssd_out_ln",
    )(y, z, norm_g, w_out_bf, xcat, modcat, ln_g, ln_b)


def _router_kernel(x_ref, mod_ref, wrT_ref, h_ref, affT_ref):
    x = x_ref[0]
    h = x * (1.0 + mod_ref[0, 4:5, :]) + mod_ref[0, 3:4, :]
    h_ref[0] = h.astype(BF16)
    h1, h2, h3 = _split3(h)
    w1, w2, w3 = _split3(wrT_ref[...])
    lt = (_dot_nt(w1, h1) + _dot_nt(w1, h2) + _dot_nt(w2, h1)
          + _dot_nt(w2, h2) + _dot_nt(w1, h3) + _dot_nt(w3, h1))
    m = jnp.max(lt, axis=0, keepdims=True)
    e = jnp.exp(lt - m)
    affT_ref[0] = e / jnp.sum(e, axis=0, keepdims=True)


def _router(x, mod, wrT):
    b, n, d = x.shape
    t = min(ROW_TILE, n)
    return pl.pallas_call(
        _router_kernel,
        grid=(b, n // t),
        in_specs=[
            pl.BlockSpec((1, t, d), lambda i, j: (i, j, 0)),
            pl.BlockSpec((1, 8, d), lambda i, j: (i, 0, 0)),
            pl.BlockSpec((N_EXPERTS, d), lambda i, j: (0, 0)),
        ],
        out_specs=[
            pl.BlockSpec((1, t, d), lambda i, j: (i, j, 0)),
            pl.BlockSpec((1, N_EXPERTS, t), lambda i, j: (i, 0, j)),
        ],
        out_shape=[
            jax.ShapeDtypeStruct((b, n, d), BF16),
            jax.ShapeDtypeStruct((b, N_EXPERTS, n), F32),
        ],
        compiler_params=_cparams(2),
        name="moe_router",
    )(x, mod, wrT)


def _select_kernel(aff_ref, pos_ref, *, k):
    a = aff_ref[0]
    ne, r, _ = a.shape
    bits = pltpu.bitcast(a, I32)

    def count(mask):
        c = jnp.sum(jnp.where(mask, 1.0, 0.0), axis=1, keepdims=True)
        return jnp.sum(c, axis=2, keepdims=True)

    def body(i, thr):
        cand = thr | jnp.left_shift(jnp.int32(1), 30 - i)
        return jnp.where(count(bits >= cand) >= k, cand, thr)

    thr = lax.fori_loop(0, 31, body, jnp.zeros((ne, 1, 1), I32))
    gt = bits > thr
    eq = bits == thr
    need = k - count(gt)

    ci = lax.broadcasted_iota(I32, (LANES, LANES), 0)
    cj = lax.broadcasted_iota(I32, (LANES, LANES), 1)
    before = jnp.where(ci < cj, 1.0, 0.0).astype(BF16)
    ones = jnp.ones((LANES, LANES), BF16)
    ri = lax.broadcasted_iota(I32, (r, r), 0)
    rj = lax.broadcasted_iota(I32, (r, r), 1)
    rows_before = jnp.where(rj < ri, 1.0, 0.0).astype(BF16)

    def exclusive_prefix(flags):
        fb = flags.reshape(ne * r, LANES).astype(BF16)
        within = _dot(fb, before).reshape(ne, r, LANES)
        tot = _dot(fb, ones).reshape(ne, r, LANES).astype(BF16)
        rows = jnp.stack([_dot(rows_before, tot[e]) for e in range(ne)], axis=0)
        return within + rows

    eq_rank = exclusive_prefix(jnp.where(eq, 1.0, 0.0))
    sel = gt | (eq & (eq_rank < need))
    pos = exclusive_prefix(jnp.where(sel, 1.0, 0.0))
    pos_ref[0] = jnp.where(sel, pos, -1.0)


def _select(affT, k):
    b, ne, n = affT.shape
    if n < SEL_LEN:
        affT = jnp.pad(affT, ((0, 0), (0, 0), (0, SEL_LEN - n)), constant_values=-1.0)
    assert affT.shape[2] == SEL_LEN
    r = SEL_LEN // LANES
    pos = pl.pallas_call(
        functools.partial(_select_kernel, k=k),
        grid=(b,),
        in_specs=[pl.BlockSpec((1, ne, r, LANES), lambda i: (i, 0, 0, 0))],
        out_specs=pl.BlockSpec((1, ne, r, LANES), lambda i: (i, 0, 0, 0)),
        out_shape=jax.ShapeDtypeStruct((b, ne, r, LANES), F32),
        compiler_params=_cparams(1),
        name="moe_select",
    )(affT.reshape(b, ne, r, LANES))
    return pos.reshape(b, ne, SEL_LEN)[:, :, :n]


def _gather_kernel(lo_ref, nsub_ref, h_ref, pos_ref, xe_ref, acc_ref, *, win):
    t = pl.program_id(2)
    ntb = pl.num_programs(2)
    idx = (pl.program_id(0) * pl.num_programs(1) + pl.program_id(1)) * ntb + t
    cap = acc_ref.shape[0]

    @pl.when(t == 0)
    def _():
        acc_ref[...] = jnp.zeros(acc_ref.shape, F32)

    lo0 = lo_ref[idx]
    h = h_ref[0]
    pos = pos_ref[0, 0]

    def body(s, carry):
        lo = lo0 + s * win
        w0 = pl.multiple_of(jnp.minimum(lo, cap - win), SUBLANES)
        slot = w0 + lax.broadcasted_iota(I32, (win, 1), 0)
        slot = jnp.where(slot >= lo, slot, -2).astype(F32)
        onehot = jnp.where(pos == slot, 1.0, 0.0).astype(BF16)
        acc_ref[pl.ds(w0, win), :] += _dot(onehot, h)
        return carry

    lax.fori_loop(0, nsub_ref[idx], body, 0)

    @pl.when(t == ntb - 1)
    def _():
        xe_ref[0, 0] = acc_ref[...].astype(BF16)


def _gather(h_bf, pos, tables, cap, tb, win):
    b, n, d = h_bf.shape
    ne = pos.shape[1]
    ntb = n // tb
    lo, nsub = tables
    grid_spec = pltpu.PrefetchScalarGridSpec(
        num_scalar_prefetch=2,
        grid=(b, ne, ntb),
        in_specs=[
            pl.BlockSpec((1, tb, d), lambda i, e, t, a0, a1: (i, t, 0)),
            pl.BlockSpec((1, 1, 1, tb), lambda i, e, t, a0, a1: (i, e, 0, t)),
        ],
        out_specs=pl.BlockSpec((1, 1, cap, d), lambda i, e, t, a0, a1: (i, e, 0, 0)),
        scratch_shapes=[pltpu.VMEM((cap, d), F32)],
    )
    return pl.pallas_call(
        functools.partial(_gather_kernel, win=win),
        grid_spec=grid_spec,
        out_shape=jax.ShapeDtypeStruct((b, ne, cap, d), BF16),
        compiler_params=_cparams(3),
        name="moe_gather",
    )(lo, nsub, h_bf, pos.reshape(b, ne, 1, n))


def _ffn_kernel(*refs, n_groups):
    xe_refs = refs[:n_groups]
    wg_ref, wu_ref, wd_ref = refs[n_groups:n_groups + 3]
    ye_refs = refs[n_groups + 3:2 * n_groups + 3]
    acc_refs = refs[2 * n_groups + 3:]
    f = pl.program_id(2)

    @pl.when(f == 0)
    def _():
        for acc_ref in acc_refs:
            acc_ref[...] = jnp.zeros(acc_ref.shape, F32)

    wg = wg_ref[0, 0].astype(BF16)
    wu = wu_ref[0, 0].astype(BF16)
    wd = wd_ref[0, 0].astype(BF16)

    for xe_ref, acc_ref in zip(xe_refs, acc_refs):
        cap = xe_ref.shape[2]
        rows = min(FFN_ROWS, cap)

        def body(r, carry, xe_ref=xe_ref, acc_ref=acc_ref, rows=rows):
            r0 = pl.multiple_of(r * rows, rows)
            x = xe_ref[0, 0, pl.ds(r0, rows), :]
            hid = (_silu(_dot(x, wg)) * _dot(x, wu)).astype(BF16)
            acc_ref[pl.ds(r0, rows), :] += _dot(hid, wd)
            return carry

        lax.fori_loop(0, cap // rows, body, 0)

    @pl.when(f == pl.num_programs(2) - 1)
    def _():
        for ye_ref, acc_ref in zip(ye_refs, acc_refs):
            ye_ref[0, 0] = acc_ref[...].astype(BF16)


def _expert_ffn(xes, w_gate, w_up, w_down, layer):
    b, ne, _, d = xes[0].shape
    f = w_gate.shape[3]
    tf = FFN_TF
    n_groups = len(xes)
    slot_spec = lambda cap: pl.BlockSpec((1, 1, cap, d), lambda i, e, j: (i, e, 0, 0))
    return pl.pallas_call(
        functools.partial(_ffn_kernel, n_groups=n_groups),
        grid=(b, ne, f // tf),
        in_specs=[slot_spec(xe.shape[2]) for xe in xes] + [
            pl.BlockSpec((1, 1, d, tf), lambda i, e, j: (layer, e, 0, j)),
            pl.BlockSpec((1, 1, d, tf), lambda i, e, j: (layer, e, 0, j)),
            pl.BlockSpec((1, 1, tf, d), lambda i, e, j: (layer, e, j, 0)),
        ],
        out_specs=[slot_spec(xe.shape[2]) for xe in xes],
        out_shape=[jax.ShapeDtypeStruct(xe.shape, BF16) for xe in xes],
        scratch_shapes=[pltpu.VMEM((xe.shape[2], d), F32) for xe in xes],
        compiler_params=_cparams(3),
        name="moe_ffn",
    )(*xes, w_gate, w_up, w_down)


def _scatter_kernel(lo_ref, nsub_ref, x_ref, mod_ref, pos_ref, aff_ref, ye_ref, lng_ref, lnb_ref, o_ref, acc_ref,
                    *, win):
    e = pl.program_id(2)
    ne = pl.num_programs(2)
    idx = (pl.program_id(0) * pl.num_programs(1) + pl.program_id(1)) * ne + e
    cap = ye_ref.shape[2]

    @pl.when(e == 0)
    def _():
        acc_ref[...] = jnp.zeros(acc_ref.shape, F32)

    lane = lax.broadcasted_iota(I32, (1, ne), 1)
    pos_e = jnp.sum(jnp.where(lane == e, pos_ref[0], 0.0), axis=1, keepdims=True)
    gate_e = jnp.sum(jnp.where(lane == e, aff_ref[0], 0.0), axis=1, keepdims=True)
    lo0 = lo_ref[idx]

    def body(s, carry):
        lo = lo0 + s * win
        w0 = pl.multiple_of(jnp.minimum(lo, cap - win), BF16_SUBLANES)
        slot = w0 + lax.broadcasted_iota(I32, (1, win), 1)
        slot = jnp.where(slot >= lo, slot, -2).astype(F32)
        onehot = jnp.where(pos_e == slot, 1.0, 0.0).astype(BF16)
        acc_ref[...] += gate_e * _dot(onehot, ye_ref[0, 0, pl.ds(w0, win), :])
        return carry

    lax.fori_loop(0, nsub_ref[idx], body, 0)

    @pl.when(e == ne - 1)
    def _():
        o_ref[0] = _ln_residual(x_ref[0], acc_ref[...], mod_ref[0, 5:6, :], lng_ref[...], lnb_ref[...])


def _scatter_ln(x, mod, pos_tok, aff_tok, ye, tables, ln_g, ln_b, tb, win):
    b, n, d = x.shape
    ne, cap = ye.shape[1], ye.shape[2]
    ntb = n // tb
    lo, nsub = tables
    grid_spec = pltpu.PrefetchScalarGridSpec(
        num_scalar_prefetch=2,
        grid=(b, ntb, ne),
        in_specs=[
            pl.BlockSpec((1, tb, d), lambda i, t, e, a0, a1: (i, t, 0)),
            pl.BlockSpec((1, 8, d), lambda i, t, e, a0, a1: (i, 0, 0)),
            pl.BlockSpec((1, tb, ne), lambda i, t, e, a0, a1: (i, t, 0)),
            pl.BlockSpec((1, tb, ne), lambda i, t, e, a0, a1: (i, t, 0)),
            pl.BlockSpec((1, 1, cap, d), lambda i, t, e, a0, a1: (i, e, 0, 0)),
            pl.BlockSpec((1, d), lambda i, t, e, a0, a1: (0, 0)),
            pl.BlockSpec((1, d), lambda i, t, e, a0, a1: (0, 0)),
        ],
        out_specs=pl.BlockSpec((1, tb, d), lambda i, t, e, a0, a1: (i, t, 0)),
        scratch_shapes=[pltpu.VMEM((tb, d), F32)],
    )
    return pl.pallas_call(
        functools.partial(_scatter_kernel, win=win),
        grid_spec=grid_spec,
        out_shape=jax.ShapeDtypeStruct((b, n, d), F32),
        compiler_params=_cparams(3),
        name="moe_scatter_ln",
    )(lo, nsub, x, mod, pos_tok, aff_tok, ye, ln_g, ln_b)


def _window_tables(pos, tb, win, align, token_major):
    b, ne, n = pos.shape
    ntb = n // tb
    cnt = jnp.sum((pos >= 0).reshape(b, ne, ntb, tb), axis=-1).astype(I32)
    start = jnp.cumsum(cnt, axis=-1) - cnt
    lo = (start // align) * align
    nsub = jnp.where(cnt > 0, (start - lo + cnt + win - 1) // win, 0)
    if token_major:
        lo, nsub = jnp.swapaxes(lo, 1, 2), jnp.swapaxes(nsub, 1, 2)
    return lo.reshape(-1).astype(I32), nsub.reshape(-1).astype(I32)


def _moe_layer(streams, wrT, w_gate, w_up, w_down, layer, ln_g, ln_b):
    routed = []
    for x, mod in streams:
        n = x.shape[1]
        cap = EC_CAPACITY_FACTOR * n // N_EXPERTS
        tb = min(MOE_TB, n)
        gwin = min(GATHER_WIN, cap)
        h_bf, affT = _router(x, mod, wrT)
        pos = _select(affT, cap)
        xe = _gather(h_bf, pos, _window_tables(pos, tb, gwin, SUBLANES, False), cap, tb, gwin)
        routed.append((affT, pos, xe, cap, tb))
    yes = _expert_ffn([r[2] for r in routed], w_gate, w_up, w_down, layer)
    outs = []
    for (x, mod), (affT, pos, _, cap, tb), ye in zip(streams, routed, yes):
        swin = min(SCATTER_WIN, cap)
        pos_tok = jnp.swapaxes(pos, 1, 2)
        aff_tok = jnp.swapaxes(affT, 1, 2)
        outs.append(_scatter_ln(x, mod, pos_tok, aff_tok, ye, _window_tables(pos, tb, swin, BF16_SUBLANES, True),
                                ln_g, ln_b, tb, swin))
    return outs


def _rope_tables_T(n_tokens):
    rows = n_tokens // GRID_W
    row_idx = jnp.repeat(jnp.arange(rows, dtype=I32), GRID_W).astype(F32)
    col_idx = jnp.tile(jnp.arange(GRID_W, dtype=I32), rows).astype(F32)
    inv_freq = ROPE_THETA ** (-jnp.arange(ROPE_FREQS, dtype=F32) / ROPE_FREQS)
    ang = jnp.concatenate([inv_freq[:, None] * row_idx[None, :], inv_freq[:, None] * col_idx[None, :]], axis=0)
    return jnp.cos(ang), jnp.sin(ang)


def kernel(x, c, ctx, c_ctx, w_mod, b_mod, ln1_g, ln1_b, ln2_g, ln2_b, attn_w_qkv, attn_w_o, attn_q_g, attn_k_g, ssd_w_in, ssd_conv_w, ssd_conv_b, ssd_dt_bias, ssd_a_log, ssd_d, ssd_norm_g, ssd_w_out, pool_w, pool_scale, moe_router, moe_w_gate, moe_w_up, moe_w_down):
    b, n, d = x.shape
    lc = ctx.shape[1]
    assert d == D_MODEL and lc == SSD_TILE and b + 1 <= 8
    assert n % ROW_TILE == 0 and n % K_CHUNK == 0 and n <= SEL_LEN

    cond8 = jnp.concatenate([c, c_ctx[None, :], jnp.zeros((8 - b - 1, d), F32)], axis=0)
    mods = _modulation(cond8, w_mod, b_mod).reshape(DEPTH, 8, 6, d)
    mods = jnp.pad(mods, ((0, 0), (0, 0), (0, 2), (0, 0)))
    cosT, sinT = _rope_tables_T(n)
    cos_ctx = jnp.ones((ROPE_HALF, lc), F32)
    sin_ctx = jnp.zeros((ROPE_HALF, lc), F32)

    for i in range(DEPTH):
        last = i == DEPTH - 1
        kind, j = i % 3, i // 3
        mod_lat = mods[i, :b]
        mod_ctx = jnp.broadcast_to(mods[i, b][None], (b, 8, d))
        l1g, l1b = ln1_g[i][None, :], ln1_b[i][None, :]
        l2g, l2b = ln2_g[i][None, :], ln2_b[i][None, :]
        if kind == 0:
            w_bf = attn_w_qkv[j].astype(BF16)
            woT_bf = attn_w_o[j].T.astype(BF16)
            qg = attn_q_g[j][:, None]
            kg = attn_k_g[j][:, None]
            qT_l, k_l, vT_l = _qkv_project(x, mod_lat, w_bf, qg, kg, cosT, sinT)
            qT_c, k_c, vT_c = _qkv_project(ctx, mod_ctx, w_bf, qg, kg, cos_ctx, sin_ctx)
            bounded = _scores_bounded(attn_q_g[j], attn_k_g[j])
            oT_l = _flash_attention(bounded, qT_l, k_c, vT_c, k_l, vT_l)
            x = _attn_out(oT_l, woT_bf, x, mod_lat, l1g, l1b)
            if not last:
                oT_c = _flash_attention(bounded, qT_c, k_c, vT_c)
                ctx = _attn_out(oT_c, woT_bf, ctx, mod_ctx, l1g, l1b)
        elif kind == 1:
            xcat = jnp.concatenate([ctx, x], axis=1)
            modcat = jnp.stack([mod_ctx, mod_lat], axis=1)
            w_in = ssd_w_in[j]
            wz = w_in[:, :D_INNER].astype(BF16)
            wx = w_in[:, D_INNER:D_INNER + CONV_DIM].astype(BF16)
            wdt = jnp.pad(w_in[:, D_INNER + CONV_DIM:], ((0, 0), (0, LANES - 2 * SSM_HEADS))).astype(BF16)
            z, xbc, dt_raw = _ssd_in_proj(xcat, modcat, wz, wx, wdt)
            conv_w8 = jnp.pad(ssd_conv_w[j], ((0, 8 - CONV_WIDTH), (0, 0)))
            xs, bm, cm = _ssd_conv(xbc, conv_w8, ssd_conv_b[j][None, :])
            a = -jnp.exp(ssd_a_log[j].astype(F32))
            dtb = ssd_dt_bias[j].astype(F32)
            dsk = jnp.repeat(ssd_d[j], SSM_HEADDIM)[None, :]
            dt_f = dt_raw[:, :, :SSM_HEADS]
            dt_b = dt_raw[:, :, SSM_HEADS:2 * SSM_HEADS]
            y_f = _ssd_scan(xs, bm, cm, dt_f, jnp.swapaxes(dt_f, 1, 2), a[0], dtb[0], False)
            y = _ssd_scan(xs, bm, cm, dt_b, jnp.swapaxes(dt_b, 1, 2), a[1], dtb[1], True, y_f, dsk)
            xcat = _ssd_out(y, z, ssd_norm_g[j][None, :], ssd_w_out[j].astype(BF16), xcat, modcat, l1g, l1b)
            ctx, x = xcat[:, :lc], xcat[:, lc:]
        else:
            pw = pool_w[j].astype(BF16)
            ps = pool_scale[j][None, :]
            x = _pool_mixer(x, mod_lat, pw, ps, l1g, l1b)
            if not last:
                ctx = _pool_mixer(ctx, mod_ctx, pw, ps, l1g, l1b)
        wrT = moe_router[i].T
        if last:
            x, = _moe_layer([(x, mod_lat)], wrT, moe_w_gate, moe_w_up, moe_w_down, i, l2g, l2b)
        else:
            x, ctx = _moe_layer([(x, mod_lat), (ctx, mod_ctx)], wrT, moe_w_gate, moe_w_up, moe_w_down, i, l2g, l2b)
    return x
```

```python
import functools
import math

import jax
import jax.numpy as jnp
from jax import lax
from jax.experimental import pallas as pl
from jax.experimental.pallas import tpu as pltpu

F32 = jnp.float32
BF16 = jnp.bfloat16
I32 = jnp.int32

D_MODEL = 1024
DEPTH = 4
GRID_W = 64
N_HEADS = 16
N_KV_HEADS = 4
GQA_GROUP = N_HEADS // N_KV_HEADS
HEAD_DIM = 64
ROPE_HALF = HEAD_DIM // 2
ROPE_FREQS = HEAD_DIM // 4
ROPE_THETA = 10000.0
D_INNER = 2048
SSM_HEADDIM = 64
SSM_HEADS = 32
SSM_GROUPS = 4
SSM_STATE = 128
CONV_WIDTH = 5
CONV_DIM = D_INNER + 2 * SSM_GROUPS * SSM_STATE
SSD_CHUNK = 128
POOL_WINDOWS = (2, 4, 8, 16)
POOL_GROUP_DIM = D_MODEL // 4
N_EXPERTS = 16
D_EXPERT = 2048
EC_CAPACITY_FACTOR = 2
DEEPNORM_ALPHA = (2.0 * DEPTH) ** 0.25
LN_EPS = 1e-5
RMS_EPS = 1e-6

LANES = 128
SUBLANES = 8
VMEM_LIMIT = 56 * 1024 * 1024

ROW_TILE = 512
SSD_TILE = 256
Q_TILE = 256
K_CHUNK = 512
PAIR_UNROLL = 3
MOE_TB = 1024
GATHER_WIN = 128
SCATTER_WIN = 256
BF16_SUBLANES = 16
FFN_TF = 512
FFN_ROWS = 2048
SEL_LEN = 16384
HALO = 8
Q_SCALE = (HEAD_DIM ** -0.5) * math.log2(math.e)
NEG_BIG = -1e30
MXU_COLS = 256
SCORE_BOUND = 60.0
BF16_SLACK = 1.01


def _cparams(n_axes):
    return pltpu.CompilerParams(dimension_semantics=("arbitrary",) * n_axes, vmem_limit_bytes=VMEM_LIMIT)


def _dot(a, b):
    return jnp.dot(a, b, preferred_element_type=F32)


def _dot_nt(a, b):
    return lax.dot_general(a, b, (((1,), (1,)), ((), ())), preferred_element_type=F32)


def _split2(a):
    hi = a.astype(BF16)
    lo = (a - hi.astype(F32)).astype(BF16)
    return hi, lo


def _split3(a):
    hi = a.astype(BF16)
    r = a - hi.astype(F32)
    mid = r.astype(BF16)
    lo = (r - mid.astype(F32)).astype(BF16)
    return hi, mid, lo


def _silu(v):
    return v * (1.0 / (1.0 + jnp.exp(-v)))


def _ln_residual(x, y, gate, ln_g, ln_b):
    v = DEEPNORM_ALPHA * x + gate * y
    mu = jnp.mean(v, axis=-1, keepdims=True)
    d = v - mu
    var = jnp.mean(d * d, axis=-1, keepdims=True)
    return d * lax.rsqrt(var + LN_EPS) * ln_g + ln_b


def _mod_kernel(c_ref, w_ref, b_ref, o_ref):
    s = _silu(c_ref[...])
    s_hi, s_lo = _split2(s)
    w = w_ref[0]
    w_hi, w_lo = _split2(w)
    o_ref[0] = _dot(s_hi, w_hi) + _dot(s_lo, w_hi) + _dot(s_hi, w_lo) + b_ref[0]


def _modulation(cond8, w_mod, b_mod):
    depth, d, d6 = w_mod.shape
    tn = 1536
    return pl.pallas_call(
        _mod_kernel,
        grid=(depth, d6 // tn),
        in_specs=[
            pl.BlockSpec((8, d), lambda i, j: (0, 0)),
            pl.BlockSpec((1, d, tn), lambda i, j: (i, 0, j)),
            pl.BlockSpec((1, 1, tn), lambda i, j: (i, 0, j)),
        ],
        out_specs=pl.BlockSpec((1, 8, tn), lambda i, j: (i, 0, j)),
        out_shape=jax.ShapeDtypeStruct((depth, 8, d6), F32),
        compiler_params=_cparams(2),
        name="modulation",
    )(cond8, w_mod, b_mod.reshape(depth, 1, d6))


def _qkv_kernel(x_ref, mod_ref, w_ref, qg_ref, kg_ref, cos_ref, sin_ref, qT_ref, k_ref, vT_ref):
    t = x_ref.shape[1]
    x = x_ref[0]
    h = (x * (1.0 + mod_ref[0, 1:2, :]) + mod_ref[0, 0:1, :]).astype(BF16)
    acc = _dot(h, w_ref[...])
    accT = acc.T
    cos = cos_ref[...][None]
    sin = sin_ref[...][None]

    def norm_rope(tT, n_heads, g):
        t3 = tT.reshape(n_heads, HEAD_DIM, t)
        ms = jnp.mean(t3 * t3, axis=1, keepdims=True)
        t3 = t3 * lax.rsqrt(ms + RMS_EPS) * g[None]
        t1 = t3[:, :ROPE_HALF, :]
        t2 = t3[:, ROPE_HALF:, :]
        return jnp.concatenate([t1 * cos - t2 * sin, t2 * cos + t1 * sin], axis=1)

    nq = N_HEADS * HEAD_DIM
    nk = N_KV_HEADS * HEAD_DIM
    q3 = norm_rope(accT[:nq], N_HEADS, qg_ref[...]) * Q_SCALE
    qT_ref[0] = q3.reshape(nq, t).astype(BF16)
    k3 = norm_rope(accT[nq:nq + nk], N_KV_HEADS, kg_ref[...])
    k_ref[0] = k3.reshape(nk, t).T.astype(BF16)
    vT_ref[0, 0] = accT[nq + nk:].astype(BF16)


def _qkv_project(x, mod, w_bf, qg, kg, cosT, sinT):
    b, n, d = x.shape
    t = min(ROW_TILE, n)
    nq = N_HEADS * HEAD_DIM
    nk = N_KV_HEADS * HEAD_DIM
    return pl.pallas_call(
        _qkv_kernel,
        grid=(b, n // t),
        in_specs=[
            pl.BlockSpec((1, t, d), lambda i, j: (i, j, 0)),
            pl.BlockSpec((1, 8, d), lambda i, j: (i, 0, 0)),
            pl.BlockSpec((d, nq + 2 * nk), lambda i, j: (0, 0)),
            pl.BlockSpec((HEAD_DIM, 1), lambda i, j: (0, 0)),
            pl.BlockSpec((HEAD_DIM, 1), lambda i, j: (0, 0)),
            pl.BlockSpec((ROPE_HALF, t), lambda i, j: (0, j)),
            pl.BlockSpec((ROPE_HALF, t), lambda i, j: (0, j)),
        ],
        out_specs=[
            pl.BlockSpec((1, nq, t), lambda i, j: (i, 0, j)),
            pl.BlockSpec((1, t, nk), lambda i, j: (i, j, 0)),
            pl.BlockSpec((1, 1, nk, t), lambda i, j: (i, j, 0, 0)),
        ],
        out_shape=[
            jax.ShapeDtypeStruct((b, nq, n), BF16),
            jax.ShapeDtypeStruct((b, n, nk), BF16),
            jax.ShapeDtypeStruct((b, n // t, nk, t), BF16),
        ],
        compiler_params=_cparams(2),
        name="qkv_project",
    )(x, mod, w_bf, qg, kg, cosT, sinT)


def _flash_kernel(*refs, n_lat_chunks, tk_lat):
    if n_lat_chunks:
        bounded_ref, qT_ref, kc_ref, vTc_ref, kl_ref, vTl_ref, o_ref, acc_ref, m_ref, sa_ref, sb_ref = refs
    else:
        bounded_ref, qT_ref, kc_ref, vTc_ref, o_ref, acc_ref, m_ref, sa_ref, sb_ref = refs
    g = pl.program_id(1)
    tq = qT_ref.shape[2]
    qb = qT_ref[0]
    q64 = jnp.concatenate([qb[j * HEAD_DIM:(j + 1) * HEAD_DIM, :] for j in range(GQA_GROUP)], axis=1)
    q256 = jnp.concatenate([q64] * N_KV_HEADS, axis=0).astype(F32)
    row_group = lax.broadcasted_iota(I32, (N_KV_HEADS * HEAD_DIM, 1), 0) // HEAD_DIM
    qpad = jnp.where(row_group == g, q256, 0.0).astype(BF16)
    col_tiles = [slice(c * MXU_COLS, (c + 1) * MXU_COLS) for c in range(GQA_GROUP * tq // MXU_COLS)]

    acc_ref[...] = jnp.zeros(acc_ref.shape, F32)

    def with_ones(vT_blk):
        return jnp.concatenate([vT_blk, jnp.ones((16, vT_blk.shape[1]), BF16)], axis=0)

    def produce(s_ref, k_blk):
        for c, cs in enumerate(col_tiles):
            s_ref[c, 0:k_blk.shape[0], :] = _dot(k_blk, qpad[:, cs])

    def consume(s_ref, vT_blk):
        va = with_ones(vT_blk)
        for c, cs in enumerate(col_tiles):
            p = jnp.exp2(s_ref[c, 0:vT_blk.shape[1], :]).astype(BF16)
            acc_ref[:, cs] += _dot(va, p)

    def lat_k(i):
        return kl_ref[0, pl.ds(pl.multiple_of(i * tk_lat, tk_lat), tk_lat), :]

    def run_bounded():
        produce(sa_ref, kc_ref[0])
        if n_lat_chunks == 0:
            consume(sa_ref, vTc_ref[0, 0])
            return
        produce(sb_ref, lat_k(0))
        consume(sa_ref, vTc_ref[0, 0])

        def pair(jj, carry):
            i = 2 * jj
            produce(sa_ref, lat_k(i + 1))
            consume(sb_ref, vTl_ref[0, i])
            produce(sb_ref, lat_k(i + 2))
            consume(sa_ref, vTl_ref[0, i + 1])
            return carry

        lax.fori_loop(0, (n_lat_chunks - 1) // 2, pair, 0, unroll=PAIR_UNROLL)
        if (n_lat_chunks - 1) % 2:
            produce(sa_ref, lat_k(n_lat_chunks - 1))
            consume(sb_ref, vTl_ref[0, n_lat_chunks - 2])
            consume(sa_ref, vTl_ref[0, n_lat_chunks - 1])
        else:
            consume(sb_ref, vTl_ref[0, n_lat_chunks - 1])

    def online_step(k_blk, vT_blk):
        va = with_ones(vT_blk)
        for cs in col_tiles:
            s = _dot(k_blk, qpad[:, cs])
            m_old = m_ref[:, cs]
            m_new = jnp.maximum(m_old, jnp.max(s, axis=0, keepdims=True))
            p = jnp.exp2(s - m_new).astype(BF16)
            acc_ref[:, cs] = acc_ref[:, cs] * jnp.exp2(m_old - m_new) + _dot(va, p)
            m_ref[:, cs] = m_new

    def run_online():
        m_ref[...] = jnp.full(m_ref.shape, -jnp.inf, F32)
        online_step(kc_ref[0], vTc_ref[0, 0])
        if n_lat_chunks:
            def body(i, carry):
                online_step(lat_k(i), vTl_ref[0, i])
                return carry
            lax.fori_loop(0, n_lat_chunks, body, 0)

    pl.when(bounded_ref[0] == 1)(run_bounded)
    pl.when(bounded_ref[0] == 0)(run_online)

    acc = acc_ref[...]
    o = acc[:HEAD_DIM] * (1.0 / acc[HEAD_DIM:HEAD_DIM + 1])
    for j in range(GQA_GROUP):
        o_ref[0, j * HEAD_DIM:(j + 1) * HEAD_DIM, :] = o[:, j * tq:(j + 1) * tq].astype(BF16)


def _flash_attention(bounded, qT, k_ctx, vT_ctx, k_lat=None, vT_lat=None):
    b, nq, n = qT.shape
    tq = min(Q_TILE, n)
    lc = k_ctx.shape[1]
    nk = N_KV_HEADS * HEAD_DIM
    gw = GQA_GROUP * HEAD_DIM
    in_specs = [
        pl.BlockSpec((1, gw, tq), lambda i, g, j, fl: (i, g, j)),
        pl.BlockSpec((1, lc, nk), lambda i, g, j, fl: (i, 0, 0)),
        pl.BlockSpec((1, 1, HEAD_DIM, lc), lambda i, g, j, fl: (i, 0, g, 0)),
    ]
    args = [qT, k_ctx, vT_ctx]
    n_chunks, tk = 0, 0
    if k_lat is not None:
        n_chunks, tk = vT_lat.shape[1], vT_lat.shape[3]
        in_specs += [
            pl.BlockSpec((1, k_lat.shape[1], nk), lambda i, g, j, fl: (i, 0, 0)),
            pl.BlockSpec((1, n_chunks, HEAD_DIM, tk), lambda i, g, j, fl: (i, 0, g, 0)),
        ]
        args += [k_lat, vT_lat]
    grid_spec = pltpu.PrefetchScalarGridSpec(
        num_scalar_prefetch=1,
        grid=(b, N_KV_HEADS, n // tq),
        in_specs=in_specs,
        out_specs=pl.BlockSpec((1, gw, tq), lambda i, g, j, fl: (i, g, j)),
        scratch_shapes=[
            pltpu.VMEM((HEAD_DIM + 16, GQA_GROUP * tq), F32),
            pltpu.VMEM((1, GQA_GROUP * tq), F32),
            pltpu.VMEM((GQA_GROUP * tq // MXU_COLS, max(tk, lc), MXU_COLS), F32),
            pltpu.VMEM((GQA_GROUP * tq // MXU_COLS, max(tk, lc), MXU_COLS), F32),
        ],
    )
    return pl.pallas_call(
        functools.partial(_flash_kernel, n_lat_chunks=n_chunks, tk_lat=tk),
        grid_spec=grid_spec,
        out_shape=jax.ShapeDtypeStruct((b, nq, n), BF16),
        compiler_params=_cparams(3),
        name="flash_attention",
    )(bounded, *args)


def _scores_bounded(q_g, k_g):
    bound = HEAD_DIM * jnp.max(jnp.abs(q_g)) * jnp.max(jnp.abs(k_g)) * (Q_SCALE * BF16_SLACK)
    return (bound <= SCORE_BOUND).astype(I32).reshape(1)


def _attn_out_kernel(oT_ref, woT_ref, x_ref, mod_ref, lng_ref, lnb_ref, o_ref):
    yT = _dot(woT_ref[...], oT_ref[0])
    o_ref[0] = _ln_residual(x_ref[0], yT.T, mod_ref[0, 2:3, :], lng_ref[...], lnb_ref[...])


def _attn_out(oT, woT_bf, x, mod, ln_g, ln_b):
    b, n, d = x.shape
    t = min(ROW_TILE, n)
    return pl.pallas_call(
        _attn_out_kernel,
        grid=(b, n // t),
        in_specs=[
            pl.BlockSpec((1, d, t), lambda i, j: (i, 0, j)),
            pl.BlockSpec((d, d), lambda i, j: (0, 0)),
            pl.BlockSpec((1, t, d), lambda i, j: (i, j, 0)),
            pl.BlockSpec((1, 8, d), lambda i, j: (i, 0, 0)),
            pl.BlockSpec((1, d), lambda i, j: (0, 0)),
            pl.BlockSpec((1, d), lambda i, j: (0, 0)),
        ],
        out_specs=pl.BlockSpec((1, t, d), lambda i, j: (i, j, 0)),
        out_shape=jax.ShapeDtypeStruct((b, n, d), F32),
        compiler_params=_cparams(2),
        name="attn_out_ln",
    )(oT, woT_bf, x, mod, ln_g, ln_b)


def _pool_kernel(x_ref, xp_ref, xn_ref, mod_ref, w_ref, ps_ref, lng_ref, lnb_ref, o_ref, scr_ref, *, n_tokens):
    j = pl.program_id(1)
    nt = pl.num_programs(1)
    t = x_ref.shape[1]
    sh = mod_ref[0, 0:1, :]
    sc = mod_ref[0, 1:2, :]
    x = x_ref[0]
    h = x * (1.0 + sc) + sh
    hp = xp_ref[0] * (1.0 + sc) + sh
    hn = xn_ref[0] * (1.0 + sc) + sh
    scr_ref[0:HALO, :] = jnp.where(j > 0, hp, 0.0)
    scr_ref[HALO:HALO + t, :] = h
    scr_ref[HALO + t:HALO + t + HALO, :] = jnp.where(j < nt - 1, hn, 0.0)
    tok = lax.broadcasted_iota(I32, (t, 1), 0) + j * t
    ys = []
    for gi, win in enumerate(POOL_WINDOWS):
        cols = slice(gi * POOL_GROUP_DIM, (gi + 1) * POOL_GROUP_DIM)
        half = win // 2
        acc = scr_ref[HALO - half:HALO - half + t, cols]
        for off in range(-half + 1, win - half):
            acc = acc + scr_ref[HALO + off:HALO + off + t, cols]
        lo = jnp.maximum(tok - half, 0)
        hi = jnp.minimum(tok + (win - half), n_tokens)
        cnt = (hi - lo).astype(F32)
        pooled = acc / cnt - h[:, cols]
        ys.append(_dot(pooled.astype(BF16), w_ref[gi]))
    y = jnp.concatenate(ys, axis=1) * ps_ref[...]
    o_ref[0] = _ln_residual(x, y, mod_ref[0, 2:3, :], lng_ref[...], lnb_ref[...])


def _pool_mixer(x, mod, w_bf, pool_scale, ln_g, ln_b):
    b, n, d = x.shape
    t = min(SSD_TILE, n)
    hb = t // HALO
    nb = n // HALO
    return pl.pallas_call(
        functools.partial(_pool_kernel, n_tokens=n),
        grid=(b, n // t),
        in_specs=[
            pl.BlockSpec((1, t, d), lambda i, j: (i, j, 0)),
            pl.BlockSpec((1, HALO, d), lambda i, j: (i, jnp.maximum(j * hb - 1, 0), 0)),
            pl.BlockSpec((1, HALO, d), lambda i, j: (i, jnp.minimum((j + 1) * hb, nb - 1), 0)),
            pl.BlockSpec((1, 8, d), lambda i, j: (i, 0, 0)),
            pl.BlockSpec((4, POOL_GROUP_DIM, POOL_GROUP_DIM), lambda i, j: (0, 0, 0)),
            pl.BlockSpec((1, d), lambda i, j: (0, 0)),
            pl.BlockSpec((1, d), lambda i, j: (0, 0)),
            pl.BlockSpec((1, d), lambda i, j: (0, 0)),
        ],
        out_specs=pl.BlockSpec((1, t, d), lambda i, j: (i, j, 0)),
        out_shape=jax.ShapeDtypeStruct((b, n, d), F32),
        scratch_shapes=[pltpu.VMEM((t + 2 * HALO, d), F32)],
        compiler_params=_cparams(2),
        name="pool_mixer_ln",
    )(x, x, x, mod, w_bf, pool_scale, ln_g, ln_b)


def _ssd_in_kernel(x_ref, mod_ref, wz_ref, wx_ref, wdt_ref, z_ref, xbc_ref, dt_ref):
    x = x_ref[0]
    h = (x * (1.0 + mod_ref[0, 0, 1:2, :]) + mod_ref[0, 0, 0:1, :]).astype(BF16)
    z_ref[0] = _dot(h, wz_ref[...])
    xbc_ref[0] = _dot(h, wx_ref[...])
    dt_ref[0] = _dot(h, wdt_ref[...])


def _ssd_in_proj(xcat, modcat, wz, wx, wdt):
    b, n, d = xcat.shape
    t = SSD_TILE
    return pl.pallas_call(
        _ssd_in_kernel,
        grid=(b, n // t),
        in_specs=[
            pl.BlockSpec((1, t, d), lambda i, j: (i, j, 0)),
            pl.BlockSpec((1, 1, 8, d), lambda i, j: (i, jnp.minimum(j, 1), 0, 0)),
            pl.BlockSpec((d, D_INNER), lambda i, j: (0, 0)),
            pl.BlockSpec((d, CONV_DIM), lambda i, j: (0, 0)),
            pl.BlockSpec((d, LANES), lambda i, j: (0, 0)),
        ],
        out_specs=[
            pl.BlockSpec((1, t, D_INNER), lambda i, j: (i, j, 0)),
            pl.BlockSpec((1, t, CONV_DIM), lambda i, j: (i, j, 0)),
            pl.BlockSpec((1, t, LANES), lambda i, j: (i, j, 0)),
        ],
        out_shape=[
            jax.ShapeDtypeStruct((b, n, D_INNER), F32),
            jax.ShapeDtypeStruct((b, n, CONV_DIM), F32),
            jax.ShapeDtypeStruct((b, n, LANES), F32),
        ],
        compiler_params=_cparams(2),
        name="ssd_in_proj",
    )(xcat, modcat, wz, wx, wdt)


def _ssd_conv_kernel(x_ref, xp_ref, xn_ref, w_ref, b_ref, xs_ref, bm_ref, cm_ref, scr_ref):
    j = pl.program_id(1)
    nt = pl.num_programs(1)
    t = x_ref.shape[1]
    scr_ref[0:HALO, :] = jnp.where(j > 1, xp_ref[0], 0.0)
    scr_ref[HALO:HALO + t, :] = x_ref[0]
    scr_ref[HALO + t:HALO + t + HALO, :] = jnp.where((j > 0) & (j < nt - 1), xn_ref[0], 0.0)
    pad = CONV_WIDTH // 2
    acc = b_ref[...] + w_ref[0:1, :] * scr_ref[HALO - pad:HALO - pad + t, :]
    for kk in range(1, CONV_WIDTH):
        acc = acc + w_ref[kk:kk + 1, :] * scr_ref[HALO - pad + kk:HALO - pad + kk + t, :]
    y = _silu(acc)
    nbc = SSM_GROUPS * SSM_STATE
    xs_ref[0] = y[:, :D_INNER]
    bm_ref[0] = y[:, D_INNER:D_INNER + nbc]
    cm_ref[0] = y[:, D_INNER + nbc:]


def _ssd_conv(xbc, conv_w8, conv_b):
    b, n, cdim = xbc.shape
    t = SSD_TILE
    hb = t // HALO
    nb = n // HALO
    nbc = SSM_GROUPS * SSM_STATE
    return pl.pallas_call(
        _ssd_conv_kernel,
        grid=(b, n // t),
        in_specs=[
            pl.BlockSpec((1, t, cdim), lambda i, j: (i, j, 0)),
            pl.BlockSpec((1, HALO, cdim), lambda i, j: (i, jnp.maximum(j * hb - 1, 0), 0)),
            pl.BlockSpec((1, HALO, cdim), lambda i, j: (i, jnp.minimum((j + 1) * hb, nb - 1), 0)),
            pl.BlockSpec((8, cdim), lambda i, j: (0, 0)),
            pl.BlockSpec((1, cdim), lambda i, j: (0, 0)),
        ],
        out_specs=[
            pl.BlockSpec((1, t, D_INNER), lambda i, j: (i, j, 0)),
            pl.BlockSpec((1, t, nbc), lambda i, j: (i, j, 0)),
            pl.BlockSpec((1, t, nbc), lambda i, j: (i, j, 0)),
        ],
        out_shape=[
            jax.ShapeDtypeStruct((b, n, D_INNER), F32),
            jax.ShapeDtypeStruct((b, n, nbc), F32),
            jax.ShapeDtypeStruct((b, n, nbc), F32),
        ],
        scratch_shapes=[pltpu.VMEM((t + 2 * HALO, cdim), F32)],
        compiler_params=_cparams(2),
        name="ssd_conv",
    )(xbc, xbc, xbc, conv_w8, conv_b)


def _softplus(v):
    return jnp.maximum(v, 0.0) + jnp.log(1.0 + jnp.exp(-jnp.abs(v)))


def _ssd_scan_kernel(*refs, reverse):
    if reverse:
        (xs_ref, bm_ref, cm_ref, dtc_ref, dtr_ref, arow_ref, acol_ref, brow_ref, bcol_ref,
         yprev_ref, dsk_ref, y_ref, h_ref) = refs
    else:
        (xs_ref, bm_ref, cm_ref, dtc_ref, dtr_ref, arow_ref, acol_ref, brow_ref, bcol_ref,
         y_ref, h_ref) = refs
    lc = SSD_CHUNK
    hg = SSM_HEADS // SSM_GROUPS
    gw = hg * SSM_HEADDIM

    @pl.when(pl.program_id(1) == 0)
    def _():
        h_ref[...] = jnp.zeros(h_ref.shape, F32)

    xs = xs_ref[0]
    bm = bm_ref[0]
    cm = cm_ref[0]
    dt_c = _softplus(dtc_ref[0] + brow_ref[...])
    dt_r = _softplus(dtr_ref[0] + bcol_ref[...])
    da_c = dt_c * arow_ref[...]
    da_r = dt_r * acol_ref[...]

    li = lax.broadcasted_iota(I32, (lc, lc), 0)
    si = lax.broadcasted_iota(I32, (lc, lc), 1)
    keep = (si >= li) if reverse else (si <= li)
    tri = jnp.where(keep, 1.0, 0.0).astype(BF16)
    triT = jnp.where((li >= si) if reverse else (li <= si), 1.0, 0.0).astype(BF16)
    c1, c2, c3 = _split3(da_c)
    cum_c = _dot(tri, c1) + _dot(tri, c2) + _dot(tri, c3)
    r1, r2, r3 = _split3(da_r)
    cum_r = _dot(r1, triT) + _dot(r2, triT) + _dot(r3, triT)
    end = 0 if reverse else lc - 1
    cum_end = cum_c[end:end + 1, :]

    hrow = lax.broadcasted_iota(I32, (SSM_HEADS, D_INNER), 0)
    hcol = lax.broadcasted_iota(I32, (SSM_HEADS, D_INNER), 1) // SSM_HEADDIM
    expand = jnp.where(hrow == hcol, 1.0, 0.0).astype(BF16)

    def expand_heads(v):
        v1, v2 = _split2(v)
        return _dot(v1, expand) + _dot(v2, expand)

    e_in = expand_heads(jnp.exp(cum_c))
    w_end = expand_heads(jnp.exp(cum_end - cum_c) * dt_c)
    cd = e_in[end:end + 1, :]

    lane = lax.broadcasted_iota(I32, (lc, LANES), 1)
    xs_bf = xs.astype(BF16)
    y_parts = []
    for g in range(SSM_GROUPS):
        cm_g = cm[:, g * SSM_STATE:(g + 1) * SSM_STATE].astype(BF16)
        bm_g = bm[:, g * SSM_STATE:(g + 1) * SSM_STATE]
        cb = _dot_nt(cm_g, bm_g.astype(BF16))
        h_g = h_ref[:, g * gw:(g + 1) * gw]
        y_off = _dot(cm_g, h_g.astype(BF16)) * e_in[:, g * gw:(g + 1) * gw]
        pair_out = []
        for pr in range(hg // 2):
            xs_pair = xs_bf[:, g * gw + pr * LANES:g * gw + (pr + 1) * LANES]
            outs = []
            for hh in (g * hg + 2 * pr, g * hg + 2 * pr + 1):
                diff = cum_c[:, hh:hh + 1] - cum_r[hh:hh + 1, :]
                dec = jnp.exp(jnp.where(keep, diff, NEG_BIG))
                mix = cb * dec * dt_r[hh:hh + 1, :]
                outs.append(_dot(mix.astype(BF16), xs_pair))
            pair_out.append(jnp.where(lane < SSM_HEADDIM, outs[0], outs[1]))
        y_parts.append(jnp.concatenate(pair_out, axis=1) + y_off)
        wx = (w_end[:, g * gw:(g + 1) * gw] * xs[:, g * gw:(g + 1) * gw]).astype(BF16)
        h_ref[:, g * gw:(g + 1) * gw] = h_g * cd[:, g * gw:(g + 1) * gw] + _dot(bm_g.T.astype(BF16), wx)
    y = jnp.concatenate(y_parts, axis=1)
    if reverse:
        y = y + yprev_ref[0] + dsk_ref[...] * xs
    y_ref[0] = y


def _ssd_scan(xs, bm, cm, dt_c, dt_r, a, dtb, reverse, yprev=None, dsk=None):
    b, n, _ = xs.shape
    lc = SSD_CHUNK
    nch = n // lc
    nctx = SSD_TILE // lc
    nbc = SSM_GROUPS * SSM_STATE
    if reverse:
        def cidx(j):
            return jnp.where(j < nctx, nctx - 1 - j, nch + nctx - 1 - j)
    else:
        def cidx(j):
            return j
    in_specs = [
        pl.BlockSpec((1, lc, D_INNER), lambda i, j: (i, cidx(j), 0)),
        pl.BlockSpec((1, lc, nbc), lambda i, j: (i, cidx(j), 0)),
        pl.BlockSpec((1, lc, nbc), lambda i, j: (i, cidx(j), 0)),
        pl.BlockSpec((1, lc, SSM_HEADS), lambda i, j: (i, cidx(j), 0)),
        pl.BlockSpec((1, SSM_HEADS, lc), lambda i, j: (i, 0, cidx(j))),
        pl.BlockSpec((1, SSM_HEADS), lambda i, j: (0, 0)),
        pl.BlockSpec((SSM_HEADS, 1), lambda i, j: (0, 0)),
        pl.BlockSpec((1, SSM_HEADS), lambda i, j: (0, 0)),
        pl.BlockSpec((SSM_HEADS, 1), lambda i, j: (0, 0)),
    ]
    args = [xs, bm, cm, dt_c, dt_r, a.reshape(1, -1), a.reshape(-1, 1), dtb.reshape(1, -1), dtb.reshape(-1, 1)]
    if reverse:
        in_specs += [
            pl.BlockSpec((1, lc, D_INNER), lambda i, j: (i, cidx(j), 0)),
            pl.BlockSpec((1, D_INNER), lambda i, j: (0, 0)),
        ]
        args += [yprev, dsk]
    return pl.pallas_call(
        functools.partial(_ssd_scan_kernel, reverse=reverse),
        grid=(b, nch),
        in_specs=in_specs,
        out_specs=pl.BlockSpec((1, lc, D_INNER), lambda i, j: (i, cidx(j), 0)),
        out_shape=jax.ShapeDtypeStruct((b, n, D_INNER), F32),
        scratch_shapes=[pltpu.VMEM((SSM_STATE, D_INNER), F32)],
        compiler_params=_cparams(2),
        name="ssd_scan_bwd" if reverse else "ssd_scan_fwd",
    )(*args)


def _ssd_out_kernel(y_ref, z_ref, ng_ref, w_ref, x_ref, mod_ref, lng_ref, lnb_ref, o_ref):
    y = y_ref[0] * _silu(z_ref[0])
    ms = jnp.mean(y * y, axis=-1, keepdims=True)
    yn = (y * lax.rsqrt(ms + RMS_EPS) * ng_ref[...]).astype(BF16)
    out = _dot(yn, w_ref[...])
    o_ref[0] = _ln_residual(x_ref[0], out, mod_ref[0, 0, 2:3, :], lng_ref[...], lnb_ref[...])


def _ssd_out(y, z, norm_g, w_out_bf, xcat, modcat, ln_g, ln_b):
    b, n, d = xcat.shape
    t = SSD_TILE
    return pl.pallas_call(
        _ssd_out_kernel,
        grid=(b, n // t),
        in_specs=[
            pl.BlockSpec((1, t, D_INNER), lambda i, j: (i, j, 0)),
            pl.BlockSpec((1, t, D_INNER), lambda i, j: (i, j, 0)),
            pl.BlockSpec((1, D_INNER), lambda i, j: (0, 0)),
            pl.BlockSpec((D_INNER, d), lambda i, j: (0, 0)),
            pl.BlockSpec((1, t, d), lambda i, j: (i, j, 0)),
            pl.BlockSpec((1, 1, 8, d), lambda i, j: (i, jnp.minimum(j, 1), 0, 0)),
            pl.BlockSpec((1, d), lambda i, j: (0, 0)),
            pl.BlockSpec((1, d), lambda i, j: (0, 0)),
        ],
        out_specs=pl.BlockSpec((1, t, d), lambda i, j: (i, j, 0)),
        out_shape=jax.ShapeDtypeStruct((b, n, d), F32),
        compiler_params=_cparams(2),
        name="ssd_out_ln",
    )(y, z, norm_g, w_out_bf, xcat, modcat, ln_g, ln_b)


def _router_kernel(x_ref, mod_ref, wrT_ref, h_ref, affT_ref):
    x = x_ref[0]
    h = x * (1.0 + mod_ref[0, 4:5, :]) + mod_ref[0, 3:4, :]
    h_ref[0] = h.astype(BF16)
    h1, h2 = _split2(h)
    w1, w2 = _split2(wrT_ref[...])
    lt = _dot_nt(w1, h1) + _dot_nt(w1, h2) + _dot_nt(w2, h1)
    m = jnp.max(lt, axis=0, keepdims=True)
    e = jnp.exp(lt - m)
    affT_ref[0] = e / jnp.sum(e, axis=0, keepdims=True)


def _router(x, mod, wrT):
    b, n, d = x.shape
    t = min(ROW_TILE, n)
    return pl.pallas_call(
        _router_kernel,
        grid=(b, n // t),
        in_specs=[
            pl.BlockSpec((1, t, d), lambda i, j: (i, j, 0)),
            pl.BlockSpec((1, 8, d), lambda i, j: (i, 0, 0)),
            pl.BlockSpec((N_EXPERTS, d), lambda i, j: (0, 0)),
        ],
        out_specs=[
            pl.BlockSpec((1, t, d), lambda i, j: (i, j, 0)),
            pl.BlockSpec((1, N_EXPERTS, t), lambda i, j: (i, 0, j)),
        ],
        out_shape=[
            jax.ShapeDtypeStruct((b, n, d), BF16),
            jax.ShapeDtypeStruct((b, N_EXPERTS, n), F32),
        ],
        compiler_params=_cparams(2),
        name="moe_router",
    )(x, mod, wrT)


def _select_kernel(aff_ref, pos_ref, *, k):
    a = aff_ref[0]
    ne, r, _ = a.shape
    bits = pltpu.bitcast(a, I32)

    def count(mask):
        c = jnp.sum(jnp.where(mask, 1.0, 0.0), axis=1, keepdims=True)
        return jnp.sum(c, axis=2, keepdims=True)

    def body(i, thr):
        cand = thr | jnp.left_shift(jnp.int32(1), 30 - i)
        return jnp.where(count(bits >= cand) >= k, cand, thr)

    thr = lax.fori_loop(0, 31, body, jnp.zeros((ne, 1, 1), I32))
    gt = bits > thr
    eq = bits == thr
    need = k - count(gt)

    ci = lax.broadcasted_iota(I32, (LANES, LANES), 0)
    cj = lax.broadcasted_iota(I32, (LANES, LANES), 1)
    before = jnp.where(ci < cj, 1.0, 0.0).astype(BF16)
    ones = jnp.ones((LANES, LANES), BF16)
    ri = lax.broadcasted_iota(I32, (r, r), 0)
    rj = lax.broadcasted_iota(I32, (r, r), 1)
    rows_before = jnp.where(rj < ri, 1.0, 0.0).astype(BF16)

    def exclusive_prefix(flags):
        fb = flags.reshape(ne * r, LANES).astype(BF16)
        within = _dot(fb, before).reshape(ne, r, LANES)
        tot = _dot(fb, ones).reshape(ne, r, LANES).astype(BF16)
        rows = jnp.stack([_dot(rows_before, tot[e]) for e in range(ne)], axis=0)
        return within + rows

    eq_rank = exclusive_prefix(jnp.where(eq, 1.0, 0.0))
    sel = gt | (eq & (eq_rank < need))
    pos = exclusive_prefix(jnp.where(sel, 1.0, 0.0))
    pos_ref[0] = jnp.where(sel, pos, -1.0)


def _select(affT, k):
    b, ne, n = affT.shape
    if n < SEL_LEN:
        affT = jnp.pad(affT, ((0, 0), (0, 0), (0, SEL_LEN - n)), constant_values=-1.0)
    assert affT.shape[2] == SEL_LEN
    r = SEL_LEN // LANES
    pos = pl.pallas_call(
        functools.partial(_select_kernel, k=k),
        grid=(b,),
        in_specs=[pl.BlockSpec((1, ne, r, LANES), lambda i: (i, 0, 0, 0))],
        out_specs=pl.BlockSpec((1, ne, r, LANES), lambda i: (i, 0, 0, 0)),
        out_shape=jax.ShapeDtypeStruct((b, ne, r, LANES), F32),
        compiler_params=_cparams(1),
        name="moe_select",
    )(affT.reshape(b, ne, r, LANES))
    return pos.reshape(b, ne, SEL_LEN)[:, :, :n]


def _gather_kernel(lo_ref, nsub_ref, h_ref, pos_ref, xe_ref, acc_ref, *, win):
    t = pl.program_id(2)
    ntb = pl.num_programs(2)
    idx = (pl.program_id(0) * pl.num_programs(1) + pl.program_id(1)) * ntb + t
    cap = acc_ref.shape[0]

    @pl.when(t == 0)
    def _():
        acc_ref[...] = jnp.zeros(acc_ref.shape, F32)

    lo0 = lo_ref[idx]
    h = h_ref[0]
    pos = pos_ref[0, 0]

    def body(s, carry):
        lo = lo0 + s * win
        w0 = pl.multiple_of(jnp.minimum(lo, cap - win), SUBLANES)
        slot = w0 + lax.broadcasted_iota(I32, (win, 1), 0)
        slot = jnp.where(slot >= lo, slot, -2).astype(F32)
        onehot = jnp.where(pos == slot, 1.0, 0.0).astype(BF16)
        acc_ref[pl.ds(w0, win), :] += _dot(onehot, h)
        return carry

    lax.fori_loop(0, nsub_ref[idx], body, 0)

    @pl.when(t == ntb - 1)
    def _():
        xe_ref[0, 0] = acc_ref[...].astype(BF16)


def _gather(h_bf, pos, tables, cap, tb, win):
    b, n, d = h_bf.shape
    ne = pos.shape[1]
    ntb = n // tb
    lo, nsub = tables
    grid_spec = pltpu.PrefetchScalarGridSpec(
        num_scalar_prefetch=2,
        grid=(b, ne, ntb),
        in_specs=[
            pl.BlockSpec((1, tb, d), lambda i, e, t, a0, a1: (i, t, 0)),
            pl.BlockSpec((1, 1, 1, tb), lambda i, e, t, a0, a1: (i, e, 0, t)),
        ],
        out_specs=pl.BlockSpec((1, 1, cap, d), lambda i, e, t, a0, a1: (i, e, 0, 0)),
        scratch_shapes=[pltpu.VMEM((cap, d), F32)],
    )
    return pl.pallas_call(
        functools.partial(_gather_kernel, win=win),
        grid_spec=grid_spec,
        out_shape=jax.ShapeDtypeStruct((b, ne, cap, d), BF16),
        compiler_params=_cparams(3),
        name="moe_gather",
    )(lo, nsub, h_bf, pos.reshape(b, ne, 1, n))


def _ffn_kernel(*refs, n_groups):
    xe_refs = refs[:n_groups]
    wg_ref, wu_ref, wd_ref = refs[n_groups:n_groups + 3]
    ye_refs = refs[n_groups + 3:2 * n_groups + 3]
    acc_refs = refs[2 * n_groups + 3:]
    f = pl.program_id(2)

    @pl.when(f == 0)
    def _():
        for acc_ref in acc_refs:
            acc_ref[...] = jnp.zeros(acc_ref.shape, F32)

    wg = wg_ref[0, 0].astype(BF16)
    wu = wu_ref[0, 0].astype(BF16)
    wd = wd_ref[0, 0].astype(BF16)

    for xe_ref, acc_ref in zip(xe_refs, acc_refs):
        cap = xe_ref.shape[2]
        rows = min(FFN_ROWS, cap)

        def body(r, carry, xe_ref=xe_ref, acc_ref=acc_ref, rows=rows):
            r0 = pl.multiple_of(r * rows, rows)
            x = xe_ref[0, 0, pl.ds(r0, rows), :]
            hid = (_silu(_dot(x, wg)) * _dot(x, wu)).astype(BF16)
            acc_ref[pl.ds(r0, rows), :] += _dot(hid, wd)
            return carry

        lax.fori_loop(0, cap // rows, body, 0)

    @pl.when(f == pl.num_programs(2) - 1)
    def _():
        for ye_ref, acc_ref in zip(ye_refs, acc_refs):
            ye_ref[0, 0] = acc_ref[...].astype(BF16)


def _expert_ffn(xes, w_gate, w_up, w_down, layer):
    b, ne, _, d = xes[0].shape
    f = w_gate.shape[3]
    tf = FFN_TF
    n_groups = len(xes)
    slot_spec = lambda cap: pl.BlockSpec((1, 1, cap, d), lambda i, e, j: (i, e, 0, 0))
    return pl.pallas_call(
        functools.partial(_ffn_kernel, n_groups=n_groups),
        grid=(b, ne, f // tf),
        in_specs=[slot_spec(xe.shape[2]) for xe in xes] + [
            pl.BlockSpec((1, 1, d, tf), lambda i, e, j: (layer, e, 0, j)),
            pl.BlockSpec((1, 1, d, tf), lambda i, e, j: (layer, e, 0, j)),
            pl.BlockSpec((1, 1, tf, d), lambda i, e, j: (layer, e, j, 0)),
        ],
        out_specs=[slot_spec(xe.shape[2]) for xe in xes],
        out_shape=[jax.ShapeDtypeStruct(xe.shape, BF16) for xe in xes],
        scratch_shapes=[pltpu.VMEM((xe.shape[2], d), F32) for xe in xes],
        compiler_params=_cparams(3),
        name="moe_ffn",
    )(*xes, w_gate, w_up, w_down)


def _scatter_kernel(lo_ref, nsub_ref, x_ref, mod_ref, pos_ref, aff_ref, ye_ref, lng_ref, lnb_ref, o_ref, acc_ref,
                    *, win):
    e = pl.program_id(2)
    ne = pl.num_programs(2)
    idx = (pl.program_id(0) * pl.num_programs(1) + pl.program_id(1)) * ne + e
    cap = ye_ref.shape[2]

    @pl.when(e == 0)
    def _():
        acc_ref[...] = jnp.zeros(acc_ref.shape, F32)

    lane = lax.broadcasted_iota(I32, (1, ne), 1)
    pos_e = jnp.sum(jnp.where(lane == e, pos_ref[0], 0.0), axis=1, keepdims=True)
    gate_e = jnp.sum(jnp.where(lane == e, aff_ref[0], 0.0), axis=1, keepdims=True)
    lo0 = lo_ref[idx]

    def body(s, carry):
        lo = lo0 + s * win
        w0 = pl.multiple_of(jnp.minimum(lo, cap - win), BF16_SUBLANES)
        slot = w0 + lax.broadcasted_iota(I32, (1, win), 1)
        slot = jnp.where(slot >= lo, slot, -2).astype(F32)
        onehot = jnp.where(pos_e == slot, 1.0, 0.0).astype(BF16)
        acc_ref[...] += gate_e * _dot(onehot, ye_ref[0, 0, pl.ds(w0, win), :])
        return carry

    lax.fori_loop(0, nsub_ref[idx], body, 0)

    @pl.when(e == ne - 1)
    def _():
        o_ref[0] = _ln_residual(x_ref[0], acc_ref[...], mod_ref[0, 5:6, :], lng_ref[...], lnb_ref[...])


def _scatter_ln(x, mod, pos_tok, aff_tok, ye, tables, ln_g, ln_b, tb, win):
    b, n, d = x.shape
    ne, cap = ye.shape[1], ye.shape[2]
    ntb = n // tb
    lo, nsub = tables
    grid_spec = pltpu.PrefetchScalarGridSpec(
        num_scalar_prefetch=2,
        grid=(b, ntb, ne),
        in_specs=[
            pl.BlockSpec((1, tb, d), lambda i, t, e, a0, a1: (i, t, 0)),
            pl.BlockSpec((1, 8, d), lambda i, t, e, a0, a1: (i, 0, 0)),
            pl.BlockSpec((1, tb, ne), lambda i, t, e, a0, a1: (i, t, 0)),
            pl.BlockSpec((1, tb, ne), lambda i, t, e, a0, a1: (i, t, 0)),
            pl.BlockSpec((1, 1, cap, d), lambda i, t, e, a0, a1: (i, e, 0, 0)),
            pl.BlockSpec((1, d), lambda i, t, e, a0, a1: (0, 0)),
            pl.BlockSpec((1, d), lambda i, t, e, a0, a1: (0, 0)),
        ],
        out_specs=pl.BlockSpec((1, tb, d), lambda i, t, e, a0, a1: (i, t, 0)),
        scratch_shapes=[pltpu.VMEM((tb, d), F32)],
    )
    return pl.pallas_call(
        functools.partial(_scatter_kernel, win=win),
        grid_spec=grid_spec,
        out_shape=jax.ShapeDtypeStruct((b, n, d), F32),
        compiler_params=_cparams(3),
        name="moe_scatter_ln",
    )(lo, nsub, x, mod, pos_tok, aff_tok, ye, ln_g, ln_b)


def _window_tables(pos, tb, win, align, token_major):
    b, ne, n = pos.shape
    ntb = n // tb
    cnt = jnp.sum((pos >= 0).reshape(b, ne, ntb, tb), axis=-1).astype(I32)
    start = jnp.cumsum(cnt, axis=-1) - cnt
    lo = (start // align) * align
    nsub = jnp.where(cnt > 0, (start - lo + cnt + win - 1) // win, 0)
    if token_major:
        lo, nsub = jnp.swapaxes(lo, 1, 2), jnp.swapaxes(nsub, 1, 2)
    return lo.reshape(-1).astype(I32), nsub.reshape(-1).astype(I32)


def _moe_layer(streams, wrT, w_gate, w_up, w_down, layer, ln_g, ln_b):
    routed = []
    for x, mod in streams:
        n = x.shape[1]
        cap = EC_CAPACITY_FACTOR * n // N_EXPERTS
        tb = min(MOE_TB, n)
        gwin = min(GATHER_WIN, cap)
        h_bf, affT = _router(x, mod, wrT)
        pos = _select(affT, cap)
        xe = _gather(h_bf, pos, _window_tables(pos, tb, gwin, SUBLANES, False), cap, tb, gwin)
        routed.append((affT, pos, xe, cap, tb))
    yes = _expert_ffn([r[2] for r in routed], w_gate, w_up, w_down, layer)
    outs = []
    for (x, mod), (affT, pos, _, cap, tb), ye in zip(streams, routed, yes):
        swin = min(SCATTER_WIN, cap)
        pos_tok = jnp.swapaxes(pos, 1, 2)
        aff_tok = jnp.swapaxes(affT, 1, 2)
        outs.append(_scatter_ln(x, mod, pos_tok, aff_tok, ye, _window_tables(pos, tb, swin, BF16_SUBLANES, True),
                                ln_g, ln_b, tb, swin))
    return outs


def _rope_tables_T(n_tokens):
    rows = n_tokens // GRID_W
    row_idx = jnp.repeat(jnp.arange(rows, dtype=I32), GRID_W).astype(F32)
    col_idx = jnp.tile(jnp.arange(GRID_W, dtype=I32), rows).astype(F32)
    inv_freq = ROPE_THETA ** (-jnp.arange(ROPE_FREQS, dtype=F32) / ROPE_FREQS)
    ang = jnp.concatenate([inv_freq[:, None] * row_idx[None, :], inv_freq[:, None] * col_idx[None, :]], axis=0)
    return jnp.cos(ang), jnp.sin(ang)


def kernel(x, c, ctx, c_ctx, w_mod, b_mod, ln1_g, ln1_b, ln2_g, ln2_b, attn_w_qkv, attn_w_o, attn_q_g, attn_k_g, ssd_w_in, ssd_conv_w, ssd_conv_b, ssd_dt_bias, ssd_a_log, ssd_d, ssd_norm_g, ssd_w_out, pool_w, pool_scale, moe_router, moe_w_gate, moe_w_up, moe_w_down):
    b, n, d = x.shape
    lc = ctx.shape[1]
    assert d == D_MODEL and lc == SSD_TILE and b + 1 <= 8
    assert n % ROW_TILE == 0 and n % K_CHUNK == 0 and n <= SEL_LEN

    cond8 = jnp.concatenate([c, c_ctx[None, :], jnp.zeros((8 - b - 1, d), F32)], axis=0)
    mods = _modulation(cond8, w_mod, b_mod).reshape(DEPTH, 8, 6, d)
    mods = jnp.pad(mods, ((0, 0), (0, 0), (0, 2), (0, 0)))
    cosT, sinT = _rope_tables_T(n)
    cos_ctx = jnp.ones((ROPE_HALF, lc), F32)
    sin_ctx = jnp.zeros((ROPE_HALF, lc), F32)

    for i in range(DEPTH):
        last = i == DEPTH - 1
        kind, j = i % 3, i // 3
        mod_lat = mods[i, :b]
        mod_ctx = jnp.broadcast_to(mods[i, b][None], (b, 8, d))
        l1g, l1b = ln1_g[i][None, :], ln1_b[i][None, :]
        l2g, l2b = ln2_g[i][None, :], ln2_b[i][None, :]
        if kind == 0:
            w_bf = attn_w_qkv[j].astype(BF16)
            woT_bf = attn_w_o[j].T.astype(BF16)
            qg = attn_q_g[j][:, None]
            kg = attn_k_g[j][:, None]
            qT_l, k_l, vT_l = _qkv_project(x, mod_lat, w_bf, qg, kg, cosT, sinT)
            qT_c, k_c, vT_c = _qkv_project(ctx, mod_ctx, w_bf, qg, kg, cos_ctx, sin_ctx)
            bounded = _scores_bounded(attn_q_g[j], attn_k_g[j])
            oT_l = _flash_attention(bounded, qT_l, k_c, vT_c, k_l, vT_l)
            x = _attn_out(oT_l, woT_bf, x, mod_lat, l1g, l1b)
            if not last:
                oT_c = _flash_attention(bounded, qT_c, k_c, vT_c)
                ctx = _attn_out(oT_c, woT_bf, ctx, mod_ctx, l1g, l1b)
        elif kind == 1:
            xcat = jnp.concatenate([ctx, x], axis=1)
            modcat = jnp.stack([mod_ctx, mod_lat], axis=1)
            w_in = ssd_w_in[j]
            wz = w_in[:, :D_INNER].astype(BF16)
            wx = w_in[:, D_INNER:D_INNER + CONV_DIM].astype(BF16)
            wdt = jnp.pad(w_in[:, D_INNER + CONV_DIM:], ((0, 0), (0, LANES - 2 * SSM_HEADS))).astype(BF16)
            z, xbc, dt_raw = _ssd_in_proj(xcat, modcat, wz, wx, wdt)
            conv_w8 = jnp.pad(ssd_conv_w[j], ((0, 8 - CONV_WIDTH), (0, 0)))
            xs, bm, cm = _ssd_conv(xbc, conv_w8, ssd_conv_b[j][None, :])
            a = -jnp.exp(ssd_a_log[j].astype(F32))
            dtb = ssd_dt_bias[j].astype(F32)
            dsk = jnp.repeat(ssd_d[j], SSM_HEADDIM)[None, :]
            dt_f = dt_raw[:, :, :SSM_HEADS]
            dt_b = dt_raw[:, :, SSM_HEADS:2 * SSM_HEADS]
            y_f = _ssd_scan(xs, bm, cm, dt_f, jnp.swapaxes(dt_f, 1, 2), a[0], dtb[0], False)
            y = _ssd_scan(xs, bm, cm, dt_b, jnp.swapaxes(dt_b, 1, 2), a[1], dtb[1], True, y_f, dsk)
            xcat = _ssd_out(y, z, ssd_norm_g[j][None, :], ssd_w_out[j].astype(BF16), xcat, modcat, l1g, l1b)
            ctx, x = xcat[:, :lc], xcat[:, lc:]
        else:
            pw = pool_w[j].astype(BF16)
            ps = pool_scale[j][None, :]
            x = _pool_mixer(x, mod_lat, pw, ps, l1g, l1b)
            if not last:
                ctx = _pool_mixer(ctx, mod_ctx, pw, ps, l1g, l1b)
        wrT = moe_router[i].T
        if last:
            x, = _moe_layer([(x, mod_lat)], wrT, moe_w_gate, moe_w_up, moe_w_down, i, l2g, l2b)
        else:
            x, ctx = _moe_layer([(x, mod_lat), (ctx, mod_ctx)], wrT, moe_w_gate, moe_w_up, moe_w_down, i, l2g, l2b)
    return x
```

```python
import functools
import math

import jax
import jax.numpy as jnp
from jax import lax
from jax.experimental import pallas as pl
from jax.experimental.pallas import tpu as pltpu

F32 = jnp.float32
BF16 = jnp.bfloat16
I32 = jnp.int32

D_MODEL = 1024
DEPTH = 4
GRID_W = 64
N_HEADS = 16
N_KV_HEADS = 4
GQA_GROUP = N_HEADS // N_KV_HEADS
HEAD_DIM = 64
ROPE_HALF = HEAD_DIM // 2
ROPE_FREQS = HEAD_DIM // 4
ROPE_THETA = 10000.0
D_INNER = 2048
SSM_HEADDIM = 64
SSM_HEADS = 32
SSM_GROUPS = 4
SSM_STATE = 128
CONV_WIDTH = 5
CONV_DIM = D_INNER + 2 * SSM_GROUPS * SSM_STATE
SSD_CHUNK = 128
POOL_WINDOWS = (2, 4, 8, 16)
POOL_GROUP_DIM = D_MODEL // 4
N_EXPERTS = 16
D_EXPERT = 2048
EC_CAPACITY_FACTOR = 2
DEEPNORM_ALPHA = (2.0 * DEPTH) ** 0.25
LN_EPS = 1e-5
RMS_EPS = 1e-6

LANES = 128
SUBLANES = 8
VMEM_LIMIT = 56 * 1024 * 1024

ROW_TILE = 512
SSD_TILE = 256
Q_TILE = 256
K_CHUNK = 512
PAIR_UNROLL = 3
MOE_TB = 1024
GATHER_EXPERTS = 2
GATHER_WIN = 128
SCATTER_WIN = 256
BF16_SUBLANES = 16
FFN_TF = 512
FFN_ROWS = 2048
SEL_LEN = 16384
HALO = 8
Q_SCALE = (HEAD_DIM ** -0.5) * math.log2(math.e)
NEG_BIG = -1e30
MXU_COLS = 256
SCORE_BOUND = 60.0
BF16_SLACK = 1.01


def _cparams(n_axes):
    return pltpu.CompilerParams(dimension_semantics=("arbitrary",) * n_axes, vmem_limit_bytes=VMEM_LIMIT)


def _dot(a, b):
    return jnp.dot(a, b, preferred_element_type=F32)


def _dot_nt(a, b):
    return lax.dot_general(a, b, (((1,), (1,)), ((), ())), preferred_element_type=F32)


def _split2(a):
    hi = a.astype(BF16)
    lo = (a - hi.astype(F32)).astype(BF16)
    return hi, lo


def _split3(a):
    hi = a.astype(BF16)
    r = a - hi.astype(F32)
    mid = r.astype(BF16)
    lo = (r - mid.astype(F32)).astype(BF16)
    return hi, mid, lo


def _silu(v):
    return v * (1.0 / (1.0 + jnp.exp(-v)))


def _ln_residual(x, y, gate, ln_g, ln_b):
    v = DEEPNORM_ALPHA * x + gate * y
    mu = jnp.mean(v, axis=-1, keepdims=True)
    d = v - mu
    var = jnp.mean(d * d, axis=-1, keepdims=True)
    return d * lax.rsqrt(var + LN_EPS) * ln_g + ln_b


def _mod_kernel(c_ref, w_ref, b_ref, o_ref):
    s = _silu(c_ref[...])
    s_hi, s_lo = _split2(s)
    w = w_ref[0]
    w_hi, w_lo = _split2(w)
    o_ref[0] = _dot(s_hi, w_hi) + _dot(s_lo, w_hi) + _dot(s_hi, w_lo) + b_ref[0]


def _modulation(cond8, w_mod, b_mod):
    depth, d, d6 = w_mod.shape
    tn = 1536
    return pl.pallas_call(
        _mod_kernel,
        grid=(depth, d6 // tn),
        in_specs=[
            pl.BlockSpec((8, d), lambda i, j: (0, 0)),
            pl.BlockSpec((1, d, tn), lambda i, j: (i, 0, j)),
            pl.BlockSpec((1, 1, tn), lambda i, j: (i, 0, j)),
        ],
        out_specs=pl.BlockSpec((1, 8, tn), lambda i, j: (i, 0, j)),
        out_shape=jax.ShapeDtypeStruct((depth, 8, d6), F32),
        compiler_params=_cparams(2),
        name="modulation",
    )(cond8, w_mod, b_mod.reshape(depth, 1, d6))


def _qkv_kernel(x_ref, mod_ref, w_ref, qg_ref, kg_ref, cos_ref, sin_ref, qT_ref, k_ref, vT_ref):
    t = x_ref.shape[1]
    x = x_ref[0]
    h = (x * (1.0 + mod_ref[0, 1:2, :]) + mod_ref[0, 0:1, :]).astype(BF16)
    acc = _dot(h, w_ref[...])
    accT = acc.T
    cos = cos_ref[...][None]
    sin = sin_ref[...][None]

    def norm_rope(tT, n_heads, g):
        t3 = tT.reshape(n_heads, HEAD_DIM, t)
        ms = jnp.mean(t3 * t3, axis=1, keepdims=True)
        t3 = t3 * lax.rsqrt(ms + RMS_EPS) * g[None]
        t1 = t3[:, :ROPE_HALF, :]
        t2 = t3[:, ROPE_HALF:, :]
        return jnp.concatenate([t1 * cos - t2 * sin, t2 * cos + t1 * sin], axis=1)

    nq = N_HEADS * HEAD_DIM
    nk = N_KV_HEADS * HEAD_DIM
    q3 = norm_rope(accT[:nq], N_HEADS, qg_ref[...]) * Q_SCALE
    qT_ref[0] = q3.reshape(nq, t).astype(BF16)
    k3 = norm_rope(accT[nq:nq + nk], N_KV_HEADS, kg_ref[...])
    k_ref[0] = k3.reshape(nk, t).T.astype(BF16)
    vT_ref[0, 0] = accT[nq + nk:].astype(BF16)


def _qkv_project(x, mod, w_bf, qg, kg, cosT, sinT):
    b, n, d = x.shape
    t = min(ROW_TILE, n)
    nq = N_HEADS * HEAD_DIM
    nk = N_KV_HEADS * HEAD_DIM
    return pl.pallas_call(
        _qkv_kernel,
        grid=(b, n // t),
        in_specs=[
            pl.BlockSpec((1, t, d), lambda i, j: (i, j, 0)),
            pl.BlockSpec((1, 8, d), lambda i, j: (i, 0, 0)),
            pl.BlockSpec((d, nq + 2 * nk), lambda i, j: (0, 0)),
            pl.BlockSpec((HEAD_DIM, 1), lambda i, j: (0, 0)),
            pl.BlockSpec((HEAD_DIM, 1), lambda i, j: (0, 0)),
            pl.BlockSpec((ROPE_HALF, t), lambda i, j: (0, j)),
            pl.BlockSpec((ROPE_HALF, t), lambda i, j: (0, j)),
        ],
        out_specs=[
            pl.BlockSpec((1, nq, t), lambda i, j: (i, 0, j)),
            pl.BlockSpec((1, t, nk), lambda i, j: (i, j, 0)),
            pl.BlockSpec((1, 1, nk, t), lambda i, j: (i, j, 0, 0)),
        ],
        out_shape=[
            jax.ShapeDtypeStruct((b, nq, n), BF16),
            jax.ShapeDtypeStruct((b, n, nk), BF16),
            jax.ShapeDtypeStruct((b, n // t, nk, t), BF16),
        ],
        compiler_params=_cparams(2),
        name="qkv_project",
    )(x, mod, w_bf, qg, kg, cosT, sinT)


def _flash_kernel(*refs, n_lat_chunks, tk_lat):
    if n_lat_chunks:
        bounded_ref, qT_ref, kc_ref, vTc_ref, kl_ref, vTl_ref, o_ref, acc_ref, m_ref, sa_ref, sb_ref = refs
    else:
        bounded_ref, qT_ref, kc_ref, vTc_ref, o_ref, acc_ref, m_ref, sa_ref, sb_ref = refs
    g = pl.program_id(1)
    tq = qT_ref.shape[2]
    qb = qT_ref[0]
    q64 = jnp.concatenate([qb[j * HEAD_DIM:(j + 1) * HEAD_DIM, :] for j in range(GQA_GROUP)], axis=1)
    q256 = jnp.concatenate([q64] * N_KV_HEADS, axis=0).astype(F32)
    row_group = lax.broadcasted_iota(I32, (N_KV_HEADS * HEAD_DIM, 1), 0) // HEAD_DIM
    qpad = jnp.where(row_group == g, q256, 0.0).astype(BF16)
    col_tiles = [slice(c * MXU_COLS, (c + 1) * MXU_COLS) for c in range(GQA_GROUP * tq // MXU_COLS)]

    acc_ref[...] = jnp.zeros(acc_ref.shape, F32)

    def with_ones(vT_blk):
        return jnp.concatenate([vT_blk, jnp.ones((16, vT_blk.shape[1]), BF16)], axis=0)

    def produce(s_ref, k_blk):
        for c, cs in enumerate(col_tiles):
            s_ref[c, 0:k_blk.shape[0], :] = _dot(k_blk, qpad[:, cs])

    def consume(s_ref, vT_blk):
        va = with_ones(vT_blk)
        for c, cs in enumerate(col_tiles):
            p = jnp.exp2(s_ref[c, 0:vT_blk.shape[1], :]).astype(BF16)
            acc_ref[:, cs] += _dot(va, p)

    def lat_k(i):
        return kl_ref[0, pl.ds(pl.multiple_of(i * tk_lat, tk_lat), tk_lat), :]

    def run_bounded():
        produce(sa_ref, kc_ref[0])
        if n_lat_chunks == 0:
            consume(sa_ref, vTc_ref[0, 0])
            return
        produce(sb_ref, lat_k(0))
        consume(sa_ref, vTc_ref[0, 0])

        def pair(jj, carry):
            i = 2 * jj
            produce(sa_ref, lat_k(i + 1))
            consume(sb_ref, vTl_ref[0, i])
            produce(sb_ref, lat_k(i + 2))
            consume(sa_ref, vTl_ref[0, i + 1])
            return carry

        lax.fori_loop(0, (n_lat_chunks - 1) // 2, pair, 0, unroll=PAIR_UNROLL)
        if (n_lat_chunks - 1) % 2:
            produce(sa_ref, lat_k(n_lat_chunks - 1))
            consume(sb_ref, vTl_ref[0, n_lat_chunks - 2])
            consume(sa_ref, vTl_ref[0, n_lat_chunks - 1])
        else:
            consume(sb_ref, vTl_ref[0, n_lat_chunks - 1])

    def online_step(k_blk, vT_blk):
        va = with_ones(vT_blk)
        for cs in col_tiles:
            s = _dot(k_blk, qpad[:, cs])
            m_old = m_ref[:, cs]
            m_new = jnp.maximum(m_old, jnp.max(s, axis=0, keepdims=True))
            p = jnp.exp2(s - m_new).astype(BF16)
            acc_ref[:, cs] = acc_ref[:, cs] * jnp.exp2(m_old - m_new) + _dot(va, p)
            m_ref[:, cs] = m_new

    def run_online():
        m_ref[...] = jnp.full(m_ref.shape, -jnp.inf, F32)
        online_step(kc_ref[0], vTc_ref[0, 0])
        if n_lat_chunks:
            def body(i, carry):
                online_step(lat_k(i), vTl_ref[0, i])
                return carry
            lax.fori_loop(0, n_lat_chunks, body, 0)

    pl.when(bounded_ref[0] == 1)(run_bounded)
    pl.when(bounded_ref[0] == 0)(run_online)

    acc = acc_ref[...]
    o = acc[:HEAD_DIM] * (1.0 / acc[HEAD_DIM:HEAD_DIM + 1])
    for j in range(GQA_GROUP):
        o_ref[0, j * HEAD_DIM:(j + 1) * HEAD_DIM, :] = o[:, j * tq:(j + 1) * tq].astype(BF16)


def _flash_attention(bounded, qT, k_ctx, vT_ctx, k_lat=None, vT_lat=None):
    b, nq, n = qT.shape
    tq = min(Q_TILE, n)
    lc = k_ctx.shape[1]
    nk = N_KV_HEADS * HEAD_DIM
    gw = GQA_GROUP * HEAD_DIM
    in_specs = [
        pl.BlockSpec((1, gw, tq), lambda i, g, j, fl: (i, g, j)),
        pl.BlockSpec((1, lc, nk), lambda i, g, j, fl: (i, 0, 0)),
        pl.BlockSpec((1, 1, HEAD_DIM, lc), lambda i, g, j, fl: (i, 0, g, 0)),
    ]
    args = [qT, k_ctx, vT_ctx]
    n_chunks, tk = 0, 0
    if k_lat is not None:
        n_chunks, tk = vT_lat.shape[1], vT_lat.shape[3]
        in_specs += [
            pl.BlockSpec((1, k_lat.shape[1], nk), lambda i, g, j, fl: (i, 0, 0)),
            pl.BlockSpec((1, n_chunks, HEAD_DIM, tk), lambda i, g, j, fl: (i, 0, g, 0)),
        ]
        args += [k_lat, vT_lat]
    grid_spec = pltpu.PrefetchScalarGridSpec(
        num_scalar_prefetch=1,
        grid=(b, N_KV_HEADS, n // tq),
        in_specs=in_specs,
        out_specs=pl.BlockSpec((1, gw, tq), lambda i, g, j, fl: (i, g, j)),
        scratch_shapes=[
            pltpu.VMEM((HEAD_DIM + 16, GQA_GROUP * tq), F32),
            pltpu.VMEM((1, GQA_GROUP * tq), F32),
            pltpu.VMEM((GQA_GROUP * tq // MXU_COLS, max(tk, lc), MXU_COLS), F32),
            pltpu.VMEM((GQA_GROUP * tq // MXU_COLS, max(tk, lc), MXU_COLS), F32),
        ],
    )
    return pl.pallas_call(
        functools.partial(_flash_kernel, n_lat_chunks=n_chunks, tk_lat=tk),
        grid_spec=grid_spec,
        out_shape=jax.ShapeDtypeStruct((b, nq, n), BF16),
        compiler_params=_cparams(3),
        name="flash_attention",
    )(bounded, *args)


def _scores_bounded(q_g, k_g):
    bound = HEAD_DIM * jnp.max(jnp.abs(q_g)) * jnp.max(jnp.abs(k_g)) * (Q_SCALE * BF16_SLACK)
    return (bound <= SCORE_BOUND).astype(I32).reshape(1)


def _attn_out_kernel(oT_ref, woT_ref, x_ref, mod_ref, lng_ref, lnb_ref, o_ref):
    yT = _dot(woT_ref[...], oT_ref[0])
    o_ref[0] = _ln_residual(x_ref[0], yT.T, mod_ref[0, 2:3, :], lng_ref[...], lnb_ref[...])


def _attn_out(oT, woT_bf, x, mod, ln_g, ln_b):
    b, n, d = x.shape
    t = min(ROW_TILE, n)
    return pl.pallas_call(
        _attn_out_kernel,
        grid=(b, n // t),
        in_specs=[
            pl.BlockSpec((1, d, t), lambda i, j: (i, 0, j)),
            pl.BlockSpec((d, d), lambda i, j: (0, 0)),
            pl.BlockSpec((1, t, d), lambda i, j: (i, j, 0)),
            pl.BlockSpec((1, 8, d), lambda i, j: (i, 0, 0)),
            pl.BlockSpec((1, d), lambda i, j: (0, 0)),
            pl.BlockSpec((1, d), lambda i, j: (0, 0)),
        ],
        out_specs=pl.BlockSpec((1, t, d), lambda i, j: (i, j, 0)),
        out_shape=jax.ShapeDtypeStruct((b, n, d), F32),
        compiler_params=_cparams(2),
        name="attn_out_ln",
    )(oT, woT_bf, x, mod, ln_g, ln_b)


def _pool_kernel(x_ref, xp_ref, xn_ref, mod_ref, w_ref, ps_ref, lng_ref, lnb_ref, o_ref, scr_ref, *, n_tokens):
    j = pl.program_id(1)
    nt = pl.num_programs(1)
    t = x_ref.shape[1]
    sh = mod_ref[0, 0:1, :]
    sc = mod_ref[0, 1:2, :]
    x = x_ref[0]
    h = x * (1.0 + sc) + sh
    hp = xp_ref[0] * (1.0 + sc) + sh
    hn = xn_ref[0] * (1.0 + sc) + sh
    scr_ref[0:HALO, :] = jnp.where(j > 0, hp, 0.0)
    scr_ref[HALO:HALO + t, :] = h
    scr_ref[HALO + t:HALO + t + HALO, :] = jnp.where(j < nt - 1, hn, 0.0)
    tok = lax.broadcasted_iota(I32, (t, 1), 0) + j * t
    ys = []
    for gi, win in enumerate(POOL_WINDOWS):
        cols = slice(gi * POOL_GROUP_DIM, (gi + 1) * POOL_GROUP_DIM)
        half = win // 2
        acc = scr_ref[HALO - half:HALO - half + t, cols]
        for off in range(-half + 1, win - half):
            acc = acc + scr_ref[HALO + off:HALO + off + t, cols]
        lo = jnp.maximum(tok - half, 0)
        hi = jnp.minimum(tok + (win - half), n_tokens)
        cnt = (hi - lo).astype(F32)
        pooled = acc / cnt - h[:, cols]
        ys.append(_dot(pooled.astype(BF16), w_ref[gi]))
    y = jnp.concatenate(ys, axis=1) * ps_ref[...]
    o_ref[0] = _ln_residual(x, y, mod_ref[0, 2:3, :], lng_ref[...], lnb_ref[...])


def _pool_mixer(x, mod, w_bf, pool_scale, ln_g, ln_b):
    b, n, d = x.shape
    t = min(SSD_TILE, n)
    hb = t // HALO
    nb = n // HALO
    return pl.pallas_call(
        functools.partial(_pool_kernel, n_tokens=n),
        grid=(b, n // t),
        in_specs=[
            pl.BlockSpec((1, t, d), lambda i, j: (i, j, 0)),
            pl.BlockSpec((1, HALO, d), lambda i, j: (i, jnp.maximum(j * hb - 1, 0), 0)),
            pl.BlockSpec((1, HALO, d), lambda i, j: (i, jnp.minimum((j + 1) * hb, nb - 1), 0)),
            pl.BlockSpec((1, 8, d), lambda i, j: (i, 0, 0)),
            pl.BlockSpec((4, POOL_GROUP_DIM, POOL_GROUP_DIM), lambda i, j: (0, 0, 0)),
            pl.BlockSpec((1, d), lambda i, j: (0, 0)),
            pl.BlockSpec((1, d), lambda i, j: (0, 0)),
            pl.BlockSpec((1, d), lambda i, j: (0, 0)),
        ],
        out_specs=pl.BlockSpec((1, t, d), lambda i, j: (i, j, 0)),
        out_shape=jax.ShapeDtypeStruct((b, n, d), F32),
        scratch_shapes=[pltpu.VMEM((t + 2 * HALO, d), F32)],
        compiler_params=_cparams(2),
        name="pool_mixer_ln",
    )(x, x, x, mod, w_bf, pool_scale, ln_g, ln_b)


def _ssd_in_kernel(x_ref, mod_ref, wz_ref, wx_ref, wdt_ref, z_ref, xbc_ref, dt_ref):
    x = x_ref[0]
    h = (x * (1.0 + mod_ref[0, 0, 1:2, :]) + mod_ref[0, 0, 0:1, :]).astype(BF16)
    z_ref[0] = _dot(h, wz_ref[...])
    xbc_ref[0] = _dot(h, wx_ref[...])
    dt_ref[0] = _dot(h, wdt_ref[...])


def _ssd_in_proj(xcat, modcat, wz, wx, wdt):
    b, n, d = xcat.shape
    t = SSD_TILE
    return pl.pallas_call(
        _ssd_in_kernel,
        grid=(b, n // t),
        in_specs=[
            pl.BlockSpec((1, t, d), lambda i, j: (i, j, 0)),
            pl.BlockSpec((1, 1, 8, d), lambda i, j: (i, jnp.minimum(j, 1), 0, 0)),
            pl.BlockSpec((d, D_INNER), lambda i, j: (0, 0)),
            pl.BlockSpec((d, CONV_DIM), lambda i, j: (0, 0)),
            pl.BlockSpec((d, LANES), lambda i, j: (0, 0)),
        ],
        out_specs=[
            pl.BlockSpec((1, t, D_INNER), lambda i, j: (i, j, 0)),
            pl.BlockSpec((1, t, CONV_DIM), lambda i, j: (i, j, 0)),
            pl.BlockSpec((1, t, LANES), lambda i, j: (i, j, 0)),
        ],
        out_shape=[
            jax.ShapeDtypeStruct((b, n, D_INNER), F32),
            jax.ShapeDtypeStruct((b, n, CONV_DIM), F32),
            jax.ShapeDtypeStruct((b, n, LANES), F32),
        ],
        compiler_params=_cparams(2),
        name="ssd_in_proj",
    )(xcat, modcat, wz, wx, wdt)


def _ssd_conv_kernel(x_ref, xp_ref, xn_ref, w_ref, b_ref, xs_ref, bm_ref, cm_ref, scr_ref):
    j = pl.program_id(1)
    nt = pl.num_programs(1)
    t = x_ref.shape[1]
    scr_ref[0:HALO, :] = jnp.where(j > 1, xp_ref[0], 0.0)
    scr_ref[HALO:HALO + t, :] = x_ref[0]
    scr_ref[HALO + t:HALO + t + HALO, :] = jnp.where((j > 0) & (j < nt - 1), xn_ref[0], 0.0)
    pad = CONV_WIDTH // 2
    acc = b_ref[...] + w_ref[0:1, :] * scr_ref[HALO - pad:HALO - pad + t, :]
    for kk in range(1, CONV_WIDTH):
        acc = acc + w_ref[kk:kk + 1, :] * scr_ref[HALO - pad + kk:HALO - pad + kk + t, :]
    y = _silu(acc)
    nbc = SSM_GROUPS * SSM_STATE
    xs_ref[0] = y[:, :D_INNER]
    bm_ref[0] = y[:, D_INNER:D_INNER + nbc]
    cm_ref[0] = y[:, D_INNER + nbc:]


def _ssd_conv(xbc, conv_w8, conv_b):
    b, n, cdim = xbc.shape
    t = SSD_TILE
    hb = t // HALO
    nb = n // HALO
    nbc = SSM_GROUPS * SSM_STATE
    return pl.pallas_call(
        _ssd_conv_kernel,
        grid=(b, n // t),
        in_specs=[
            pl.BlockSpec((1, t, cdim), lambda i, j: (i, j, 0)),
            pl.BlockSpec((1, HALO, cdim), lambda i, j: (i, jnp.maximum(j * hb - 1, 0), 0)),
            pl.BlockSpec((1, HALO, cdim), lambda i, j: (i, jnp.minimum((j + 1) * hb, nb - 1), 0)),
            pl.BlockSpec((8, cdim), lambda i, j: (0, 0)),
            pl.BlockSpec((1, cdim), lambda i, j: (0, 0)),
        ],
        out_specs=[
            pl.BlockSpec((1, t, D_INNER), lambda i, j: (i, j, 0)),
            pl.BlockSpec((1, t, nbc), lambda i, j: (i, j, 0)),
            pl.BlockSpec((1, t, nbc), lambda i, j: (i, j, 0)),
        ],
        out_shape=[
            jax.ShapeDtypeStruct((b, n, D_INNER), F32),
            jax.ShapeDtypeStruct((b, n, nbc), F32),
            jax.ShapeDtypeStruct((b, n, nbc), F32),
        ],
        scratch_shapes=[pltpu.VMEM((t + 2 * HALO, cdim), F32)],
        compiler_params=_cparams(2),
        name="ssd_conv",
    )(xbc, xbc, xbc, conv_w8, conv_b)


def _softplus(v):
    return jnp.maximum(v, 0.0) + jnp.log(1.0 + jnp.exp(-jnp.abs(v)))


def _ssd_scan_kernel(*refs, reverse):
    if reverse:
        (xs_ref, bm_ref, cm_ref, dtc_ref, dtr_ref, arow_ref, acol_ref, brow_ref, bcol_ref,
         yprev_ref, dsk_ref, y_ref, h_ref) = refs
    else:
        (xs_ref, bm_ref, cm_ref, dtc_ref, dtr_ref, arow_ref, acol_ref, brow_ref, bcol_ref,
         y_ref, h_ref) = refs
    lc = SSD_CHUNK
    hg = SSM_HEADS // SSM_GROUPS
    gw = hg * SSM_HEADDIM

    @pl.when(pl.program_id(1) == 0)
    def _():
        h_ref[...] = jnp.zeros(h_ref.shape, F32)

    xs = xs_ref[0]
    bm = bm_ref[0]
    cm = cm_ref[0]
    dt_c = _softplus(dtc_ref[0] + brow_ref[...])
    dt_r = _softplus(dtr_ref[0] + bcol_ref[...])
    da_c = dt_c * arow_ref[...]
    da_r = dt_r * acol_ref[...]

    li = lax.broadcasted_iota(I32, (lc, lc), 0)
    si = lax.broadcasted_iota(I32, (lc, lc), 1)
    keep = (si >= li) if reverse else (si <= li)
    tri = jnp.where(keep, 1.0, 0.0).astype(BF16)
    triT = jnp.where((li >= si) if reverse else (li <= si), 1.0, 0.0).astype(BF16)
    c1, c2, c3 = _split3(da_c)
    cum_c = _dot(tri, c1) + _dot(tri, c2) + _dot(tri, c3)
    r1, r2, r3 = _split3(da_r)
    cum_r = _dot(r1, triT) + _dot(r2, triT) + _dot(r3, triT)
    end = 0 if reverse else lc - 1
    cum_end = cum_c[end:end + 1, :]

    hrow = lax.broadcasted_iota(I32, (SSM_HEADS, D_INNER), 0)
    hcol = lax.broadcasted_iota(I32, (SSM_HEADS, D_INNER), 1) // SSM_HEADDIM
    expand = jnp.where(hrow == hcol, 1.0, 0.0).astype(BF16)

    def expand_heads(v):
        v1, v2 = _split2(v)
        return _dot(v1, expand) + _dot(v2, expand)

    e_in = expand_heads(jnp.exp(cum_c))
    w_end = expand_heads(jnp.exp(cum_end - cum_c) * dt_c)
    cd = e_in[end:end + 1, :]

    lane = lax.broadcasted_iota(I32, (lc, LANES), 1)
    xs_bf = xs.astype(BF16)
    y_parts = []
    for g in range(SSM_GROUPS):
        cm_g = cm[:, g * SSM_STATE:(g + 1) * SSM_STATE].astype(BF16)
        bm_g = bm[:, g * SSM_STATE:(g + 1) * SSM_STATE]
        cb = _dot_nt(cm_g, bm_g.astype(BF16))
        h_g = h_ref[:, g * gw:(g + 1) * gw]
        y_off = _dot(cm_g, h_g.astype(BF16)) * e_in[:, g * gw:(g + 1) * gw]
        pair_out = []
        for pr in range(hg // 2):
            xs_pair = xs_bf[:, g * gw + pr * LANES:g * gw + (pr + 1) * LANES]
            outs = []
            for hh in (g * hg + 2 * pr, g * hg + 2 * pr + 1):
                diff = cum_c[:, hh:hh + 1] - cum_r[hh:hh + 1, :]
                dec = jnp.exp(jnp.where(keep, diff, NEG_BIG))
                mix = cb * dec * dt_r[hh:hh + 1, :]
                outs.append(_dot(mix.astype(BF16), xs_pair))
            pair_out.append(jnp.where(lane < SSM_HEADDIM, outs[0], outs[1]))
        y_parts.append(jnp.concatenate(pair_out, axis=1) + y_off)
        wx = (w_end[:, g * gw:(g + 1) * gw] * xs[:, g * gw:(g + 1) * gw]).astype(BF16)
        h_ref[:, g * gw:(g + 1) * gw] = h_g * cd[:, g * gw:(g + 1) * gw] + _dot(bm_g.T.astype(BF16), wx)
    y = jnp.concatenate(y_parts, axis=1)
    if reverse:
        y = y + yprev_ref[0] + dsk_ref[...] * xs
    y_ref[0] = y


def _ssd_scan(xs, bm, cm, dt_c, dt_r, a, dtb, reverse, yprev=None, dsk=None):
    b, n, _ = xs.shape
    lc = SSD_CHUNK
    nch = n // lc
    nctx = SSD_TILE // lc
    nbc = SSM_GROUPS * SSM_STATE
    if reverse:
        def cidx(j):
            return jnp.where(j < nctx, nctx - 1 - j, nch + nctx - 1 - j)
    else:
        def cidx(j):
            return j
    in_specs = [
        pl.BlockSpec((1, lc, D_INNER), lambda i, j: (i, cidx(j), 0)),
        pl.BlockSpec((1, lc, nbc), lambda i, j: (i, cidx(j), 0)),
        pl.BlockSpec((1, lc, nbc), lambda i, j: (i, cidx(j), 0)),
        pl.BlockSpec((1, lc, SSM_HEADS), lambda i, j: (i, cidx(j), 0)),
        pl.BlockSpec((1, SSM_HEADS, lc), lambda i, j: (i, 0, cidx(j))),
        pl.BlockSpec((1, SSM_HEADS), lambda i, j: (0, 0)),
        pl.BlockSpec((SSM_HEADS, 1), lambda i, j: (0, 0)),
        pl.BlockSpec((1, SSM_HEADS), lambda i, j: (0, 0)),
        pl.BlockSpec((SSM_HEADS, 1), lambda i, j: (0, 0)),
    ]
    args = [xs, bm, cm, dt_c, dt_r, a.reshape(1, -1), a.reshape(-1, 1), dtb.reshape(1, -1), dtb.reshape(-1, 1)]
    if reverse:
        in_specs += [
            pl.BlockSpec((1, lc, D_INNER), lambda i, j: (i, cidx(j), 0)),
            pl.BlockSpec((1, D_INNER), lambda i, j: (0, 0)),
        ]
        args += [yprev, dsk]
    return pl.pallas_call(
        functools.partial(_ssd_scan_kernel, reverse=reverse),
        grid=(b, nch),
        in_specs=in_specs,
        out_specs=pl.BlockSpec((1, lc, D_INNER), lambda i, j: (i, cidx(j), 0)),
        out_shape=jax.ShapeDtypeStruct((b, n, D_INNER), F32),
        scratch_shapes=[pltpu.VMEM((SSM_STATE, D_INNER), F32)],
        compiler_params=_cparams(2),
        name="ssd_scan_bwd" if reverse else "ssd_scan_fwd",
    )(*args)


def _ssd_out_kernel(y_ref, z_ref, ng_ref, w_ref, x_ref, mod_ref, lng_ref, lnb_ref, o_ref):
    y = y_ref[0] * _silu(z_ref[0])
    ms = jnp.mean(y * y, axis=-1, keepdims=True)
    yn = (y * lax.rsqrt(ms + RMS_EPS) * ng_ref[...]).astype(BF16)
    out = _dot(yn, w_ref[...])
    o_ref[0] = _ln_residual(x_ref[0], out, mod_ref[0, 0, 2:3, :], lng_ref[...], lnb_ref[...])


def _ssd_out(y, z, norm_g, w_out_bf, xcat, modcat, ln_g, ln_b):
    b, n, d = xcat.shape
    t = SSD_TILE
    return pl.pallas_call(
        _ssd_out_kernel,
        grid=(b, n // t),
        in_specs=[
            pl.BlockSpec((1, t, D_INNER), lambda i, j: (i, j, 0)),
            pl.BlockSpec((1, t, D_INNER), lambda i, j: (i, j, 0)),
            pl.BlockSpec((1, D_INNER), lambda i, j: (0, 0)),
            pl.BlockSpec((D_INNER, d), lambda i, j: (0, 0)),
            pl.BlockSpec((1, t, d), lambda i, j: (i, j, 0)),
            pl.BlockSpec((1, 1, 8, d), lambda i, j: (i, jnp.minimum(j, 1), 0, 0)),
            pl.BlockSpec((1, d), lambda i, j: (0, 0)),
            pl.BlockSpec((1, d), lambda i, j: (0, 0)),
        ],
        out_specs=pl.BlockSpec((1, t, d), lambda i, j: (i, j, 0)),
        out_shape=jax.ShapeDtypeStruct((b, n, d), F32),
        compiler_params=_cparams(2),
        name="ssd_out_ln",
    )(y, z, norm_g, w_out_bf, xcat, modcat, ln_g, ln_b)


def _router_kernel(x_ref, mod_ref, wrT_ref, h_ref, affT_ref):
    x = x_ref[0]
    h = x * (1.0 + mod_ref[0, 4:5, :]) + mod_ref[0, 3:4, :]
    h_ref[0] = h.astype(BF16)
    h1, h2 = _split2(h)
    w1, w2 = _split2(wrT_ref[...])
    lt = _dot_nt(w1, h1) + _dot_nt(w1, h2) + _dot_nt(w2, h1)
    m = jnp.max(lt, axis=0, keepdims=True)
    e = jnp.exp(lt - m)
    affT_ref[0] = e / jnp.sum(e, axis=0, keepdims=True)


def _router(x, mod, wrT):
    b, n, d = x.shape
    t = min(ROW_TILE, n)
    return pl.pallas_call(
        _router_kernel,
        grid=(b, n // t),
        in_specs=[
            pl.BlockSpec((1, t, d), lambda i, j: (i, j, 0)),
            pl.BlockSpec((1, 8, d), lambda i, j: (i, 0, 0)),
            pl.BlockSpec((N_EXPERTS, d), lambda i, j: (0, 0)),
        ],
        out_specs=[
            pl.BlockSpec((1, t, d), lambda i, j: (i, j, 0)),
            pl.BlockSpec((1, N_EXPERTS, t), lambda i, j: (i, 0, j)),
        ],
        out_shape=[
            jax.ShapeDtypeStruct((b, n, d), BF16),
            jax.ShapeDtypeStruct((b, N_EXPERTS, n), F32),
        ],
        compiler_params=_cparams(2),
        name="moe_router",
    )(x, mod, wrT)


def _select_kernel(aff_ref, pos_ref, *, k):
    a = aff_ref[0]
    ne, r, _ = a.shape
    bits = pltpu.bitcast(a, I32)

    def count(mask):
        c = jnp.sum(jnp.where(mask, 1.0, 0.0), axis=1, keepdims=True)
        return jnp.sum(c, axis=2, keepdims=True)

    def body(i, thr):
        cand = thr | jnp.left_shift(jnp.int32(1), 30 - i)
        return jnp.where(count(bits >= cand) >= k, cand, thr)

    thr = lax.fori_loop(0, 31, body, jnp.zeros((ne, 1, 1), I32))
    gt = bits > thr
    eq = bits == thr
    need = k - count(gt)

    ci = lax.broadcasted_iota(I32, (LANES, LANES), 0)
    cj = lax.broadcasted_iota(I32, (LANES, LANES), 1)
    before = jnp.where(ci < cj, 1.0, 0.0).astype(BF16)
    ones = jnp.ones((LANES, LANES), BF16)
    ri = lax.broadcasted_iota(I32, (r, r), 0)
    rj = lax.broadcasted_iota(I32, (r, r), 1)
    rows_before = jnp.where(rj < ri, 1.0, 0.0).astype(BF16)

    def exclusive_prefix(flags):
        fb = flags.reshape(ne * r, LANES).astype(BF16)
        within = _dot(fb, before).reshape(ne, r, LANES)
        tot = _dot(fb, ones).reshape(ne, r, LANES).astype(BF16)
        rows = jnp.stack([_dot(rows_before, tot[e]) for e in range(ne)], axis=0)
        return within + rows

    eq_rank = exclusive_prefix(jnp.where(eq, 1.0, 0.0))
    sel = gt | (eq & (eq_rank < need))
    pos = exclusive_prefix(jnp.where(sel, 1.0, 0.0))
    pos_ref[0] = jnp.where(sel, pos, -1.0)


def _select(affT, k):
    b, ne, n = affT.shape
    if n < SEL_LEN:
        affT = jnp.pad(affT, ((0, 0), (0, 0), (0, SEL_LEN - n)), constant_values=-1.0)
    assert affT.shape[2] == SEL_LEN
    r = SEL_LEN // LANES
    pos = pl.pallas_call(
        functools.partial(_select_kernel, k=k),
        grid=(b,),
        in_specs=[pl.BlockSpec((1, ne, r, LANES), lambda i: (i, 0, 0, 0))],
        out_specs=pl.BlockSpec((1, ne, r, LANES), lambda i: (i, 0, 0, 0)),
        out_shape=jax.ShapeDtypeStruct((b, ne, r, LANES), F32),
        compiler_params=_cparams(1),
        name="moe_select",
    )(affT.reshape(b, ne, r, LANES))
    return pos.reshape(b, ne, SEL_LEN)[:, :, :n]


def _gather_kernel(lo_ref, nsub_ref, h_ref, pos_ref, xe_ref, acc_ref, *, win):
    t = pl.program_id(2)
    ntb = pl.num_programs(2)
    cap = acc_ref.shape[1]

    @pl.when(t == 0)
    def _():
        acc_ref[...] = jnp.zeros(acc_ref.shape, F32)

    h = h_ref[0]
    for k in range(GATHER_EXPERTS):
        idx = ((pl.program_id(0) * pl.num_programs(1) + pl.program_id(1)) * GATHER_EXPERTS + k) * ntb + t
        lo0 = lo_ref[idx]
        pos = pos_ref[0, k]

        def body(s, carry, k=k, lo0=lo0, pos=pos):
            lo = lo0 + s * win
            w0 = pl.multiple_of(jnp.minimum(lo, cap - win), SUBLANES)
            slot = w0 + lax.broadcasted_iota(I32, (win, 1), 0)
            slot = jnp.where(slot >= lo, slot, -2).astype(F32)
            onehot = jnp.where(pos == slot, 1.0, 0.0).astype(BF16)
            acc_ref[k, pl.ds(w0, win), :] += _dot(onehot, h)
            return carry

        lax.fori_loop(0, nsub_ref[idx], body, 0)

    @pl.when(t == ntb - 1)
    def _():
        xe_ref[0] = acc_ref[...].astype(BF16)


def _gather(h_bf, pos, tables, cap, tb, win):
    b, n, d = h_bf.shape
    ne = pos.shape[1]
    ntb = n // tb
    lo, nsub = tables
    ge = GATHER_EXPERTS
    grid_spec = pltpu.PrefetchScalarGridSpec(
        num_scalar_prefetch=2,
        grid=(b, ne // ge, ntb),
        in_specs=[
            pl.BlockSpec((1, tb, d), lambda i, e, t, a0, a1: (i, t, 0)),
            pl.BlockSpec((1, ge, 1, tb), lambda i, e, t, a0, a1: (i, e, 0, t)),
        ],
        out_specs=pl.BlockSpec((1, ge, cap, d), lambda i, e, t, a0, a1: (i, e, 0, 0)),
        scratch_shapes=[pltpu.VMEM((ge, cap, d), F32)],
    )
    return pl.pallas_call(
        functools.partial(_gather_kernel, win=win),
        grid_spec=grid_spec,
        out_shape=jax.ShapeDtypeStruct((b, ne, cap, d), BF16),
        compiler_params=_cparams(3),
        name="moe_gather",
    )(lo, nsub, h_bf, pos.reshape(b, ne, 1, n))


def _ffn_kernel(*refs, n_groups):
    xe_refs = refs[:n_groups]
    wg_ref, wu_ref, wd_ref = refs[n_groups:n_groups + 3]
    ye_refs = refs[n_groups + 3:2 * n_groups + 3]
    acc_refs = refs[2 * n_groups + 3:]
    f = pl.program_id(2)

    @pl.when(f == 0)
    def _():
        for acc_ref in acc_refs:
            acc_ref[...] = jnp.zeros(acc_ref.shape, F32)

    wg = wg_ref[0, 0].astype(BF16)
    wu = wu_ref[0, 0].astype(BF16)
    wd = wd_ref[0, 0].astype(BF16)

    for xe_ref, acc_ref in zip(xe_refs, acc_refs):
        cap = xe_ref.shape[2]
        rows = min(FFN_ROWS, cap)

        def body(r, carry, xe_ref=xe_ref, acc_ref=acc_ref, rows=rows):
            r0 = pl.multiple_of(r * rows, rows)
            x = xe_ref[0, 0, pl.ds(r0, rows), :]
            hid = (_silu(_dot(x, wg)) * _dot(x, wu)).astype(BF16)
            acc_ref[pl.ds(r0, rows), :] += _dot(hid, wd)
            return carry

        lax.fori_loop(0, cap // rows, body, 0)

    @pl.when(f == pl.num_programs(2) - 1)
    def _():
        for ye_ref, acc_ref in zip(ye_refs, acc_refs):
            ye_ref[0, 0] = acc_ref[...].astype(BF16)


def _expert_ffn(xes, w_gate, w_up, w_down, layer):
    b, ne, _, d = xes[0].shape
    f = w_gate.shape[3]
    tf = FFN_TF
    n_groups = len(xes)
    slot_spec = lambda cap: pl.BlockSpec((1, 1, cap, d), lambda i, e, j: (i, e, 0, 0))
    return pl.pallas_call(
        functools.partial(_ffn_kernel, n_groups=n_groups),
        grid=(b, ne, f // tf),
        in_specs=[slot_spec(xe.shape[2]) for xe in xes] + [
            pl.BlockSpec((1, 1, d, tf), lambda i, e, j: (layer, e, 0, j)),
            pl.BlockSpec((1, 1, d, tf), lambda i, e, j: (layer, e, 0, j)),
            pl.BlockSpec((1, 1, tf, d), lambda i, e, j: (layer, e, j, 0)),
        ],
        out_specs=[slot_spec(xe.shape[2]) for xe in xes],
        out_shape=[jax.ShapeDtypeStruct(xe.shape, BF16) for xe in xes],
        scratch_shapes=[pltpu.VMEM((xe.shape[2], d), F32) for xe in xes],
        compiler_params=_cparams(3),
        name="moe_ffn",
    )(*xes, w_gate, w_up, w_down)


def _scatter_kernel(lo_ref, nsub_ref, x_ref, mod_ref, pos_ref, aff_ref, ye_ref, lng_ref, lnb_ref, o_ref, acc_ref,
                    *, win):
    e = pl.program_id(2)
    ne = pl.num_programs(2)
    idx = (pl.program_id(0) * pl.num_programs(1) + pl.program_id(1)) * ne + e
    cap = ye_ref.shape[2]

    @pl.when(e == 0)
    def _():
        acc_ref[...] = jnp.zeros(acc_ref.shape, F32)

    lane = lax.broadcasted_iota(I32, (1, ne), 1)
    pos_e = jnp.sum(jnp.where(lane == e, pos_ref[0], 0.0), axis=1, keepdims=True)
    gate_e = jnp.sum(jnp.where(lane == e, aff_ref[0], 0.0), axis=1, keepdims=True)
    lo0 = lo_ref[idx]

    def body(s, carry):
        lo = lo0 + s * win
        w0 = pl.multiple_of(jnp.minimum(lo, cap - win), BF16_SUBLANES)
        slot = w0 + lax.broadcasted_iota(I32, (1, win), 1)
        slot = jnp.where(slot >= lo, slot, -2).astype(F32)
        onehot = jnp.where(pos_e == slot, 1.0, 0.0).astype(BF16)
        acc_ref[...] += gate_e * _dot(onehot, ye_ref[0, 0, pl.ds(w0, win), :])
        return carry

    lax.fori_loop(0, nsub_ref[idx], body, 0)

    @pl.when(e == ne - 1)
    def _():
        o_ref[0] = _ln_residual(x_ref[0], acc_ref[...], mod_ref[0, 5:6, :], lng_ref[...], lnb_ref[...])


def _scatter_ln(x, mod, pos_tok, aff_tok, ye, tables, ln_g, ln_b, tb, win):
    b, n, d = x.shape
    ne, cap = ye.shape[1], ye.shape[2]
    ntb = n // tb
    lo, nsub = tables
    grid_spec = pltpu.PrefetchScalarGridSpec(
        num_scalar_prefetch=2,
        grid=(b, ntb, ne),
        in_specs=[
            pl.BlockSpec((1, tb, d), lambda i, t, e, a0, a1: (i, t, 0)),
            pl.BlockSpec((1, 8, d), lambda i, t, e, a0, a1: (i, 0, 0)),
            pl.BlockSpec((1, tb, ne), lambda i, t, e, a0, a1: (i, t, 0)),
            pl.BlockSpec((1, tb, ne), lambda i, t, e, a0, a1: (i, t, 0)),
            pl.BlockSpec((1, 1, cap, d), lambda i, t, e, a0, a1: (i, e, 0, 0)),
            pl.BlockSpec((1, d), lambda i, t, e, a0, a1: (0, 0)),
            pl.BlockSpec((1, d), lambda i, t, e, a0, a1: (0, 0)),
        ],
        out_specs=pl.BlockSpec((1, tb, d), lambda i, t, e, a0, a1: (i, t, 0)),
        scratch_shapes=[pltpu.VMEM((tb, d), F32)],
    )
    return pl.pallas_call(
        functools.partial(_scatter_kernel, win=win),
        grid_spec=grid_spec,
        out_shape=jax.ShapeDtypeStruct((b, n, d), F32),
        compiler_params=_cparams(3),
        name="moe_scatter_ln",
    )(lo, nsub, x, mod, pos_tok, aff_tok, ye, ln_g, ln_b)


def _window_tables(pos, tb, win, align, token_major):
    b, ne, n = pos.shape
    ntb = n // tb
    cnt = jnp.sum((pos >= 0).reshape(b, ne, ntb, tb), axis=-1).astype(I32)
    start = jnp.cumsum(cnt, axis=-1) - cnt
    lo = (start // align) * align
    nsub = jnp.where(cnt > 0, (start - lo + cnt + win - 1) // win, 0)
    if token_major:
        lo, nsub = jnp.swapaxes(lo, 1, 2), jnp.swapaxes(nsub, 1, 2)
    return lo.reshape(-1).astype(I32), nsub.reshape(-1).astype(I32)


def _moe_layer(streams, wrT, w_gate, w_up, w_down, layer, ln_g, ln_b):
    routed = []
    for x, mod in streams:
        n = x.shape[1]
        cap = EC_CAPACITY_FACTOR * n // N_EXPERTS
        tb = min(MOE_TB, n)
        gwin = min(GATHER_WIN, cap)
        h_bf, affT = _router(x, mod, wrT)
        pos = _select(affT, cap)
        xe = _gather(h_bf, pos, _window_tables(pos, tb, gwin, SUBLANES, False), cap, tb, gwin)
        routed.append((affT, pos, xe, cap, tb))
    yes = _expert_ffn([r[2] for r in routed], w_gate, w_up, w_down, layer)
    outs = []
    for (x, mod), (affT, pos, _, cap, tb), ye in zip(streams, routed, yes):
        swin = min(SCATTER_WIN, cap)
        pos_tok = jnp.swapaxes(pos, 1, 2)
        aff_tok = jnp.swapaxes(affT, 1, 2)
        outs.append(_scatter_ln(x, mod, pos_tok, aff_tok, ye, _window_tables(pos, tb, swin, BF16_SUBLANES, True),
                                ln_g, ln_b, tb, swin))
    return outs


def _rope_tables_T(n_tokens):
    rows = n_tokens // GRID_W
    row_idx = jnp.repeat(jnp.arange(rows, dtype=I32), GRID_W).astype(F32)
    col_idx = jnp.tile(jnp.arange(GRID_W, dtype=I32), rows).astype(F32)
    inv_freq = ROPE_THETA ** (-jnp.arange(ROPE_FREQS, dtype=F32) / ROPE_FREQS)
    ang = jnp.concatenate([inv_freq[:, None] * row_idx[None, :], inv_freq[:, None] * col_idx[None, :]], axis=0)
    return jnp.cos(ang), jnp.sin(ang)


def kernel(x, c, ctx, c_ctx, w_mod, b_mod, ln1_g, ln1_b, ln2_g, ln2_b, attn_w_qkv, attn_w_o, attn_q_g, attn_k_g, ssd_w_in, ssd_conv_w, ssd_conv_b, ssd_dt_bias, ssd_a_log, ssd_d, ssd_norm_g, ssd_w_out, pool_w, pool_scale, moe_router, moe_w_gate, moe_w_up, moe_w_down):
    b, n, d = x.shape
    lc = ctx.shape[1]
    assert d == D_MODEL and lc == SSD_TILE and b + 1 <= 8
    assert n % ROW_TILE == 0 and n % K_CHUNK == 0 and n <= SEL_LEN

    cond8 = jnp.concatenate([c, c_ctx[None, :], jnp.zeros((8 - b - 1, d), F32)], axis=0)
    mods = _modulation(cond8, w_mod, b_mod).reshape(DEPTH, 8, 6, d)
    mods = jnp.pad(mods, ((0, 0), (0, 0), (0, 2), (0, 0)))
    cosT, sinT = _rope_tables_T(n)
    cos_ctx = jnp.ones((ROPE_HALF, lc), F32)
    sin_ctx = jnp.zeros((ROPE_HALF, lc), F32)

    for i in range(DEPTH):
        last = i == DEPTH - 1
        kind, j = i % 3, i // 3
        mod_lat = mods[i, :b]
        mod_ctx = jnp.broadcast_to(mods[i, b][None], (b, 8, d))
        l1g, l1b = ln1_g[i][None, :], ln1_b[i][None, :]
        l2g, l2b = ln2_g[i][None, :], ln2_b[i][None, :]
        if kind == 0:
            w_bf = attn_w_qkv[j].astype(BF16)
            woT_bf = attn_w_o[j].T.astype(BF16)
            qg = attn_q_g[j][:, None]
            kg = attn_k_g[j][:, None]
            qT_l, k_l, vT_l = _qkv_project(x, mod_lat, w_bf, qg, kg, cosT, sinT)
            qT_c, k_c, vT_c = _qkv_project(ctx, mod_ctx, w_bf, qg, kg, cos_ctx, sin_ctx)
            bounded = _scores_bounded(attn_q_g[j], attn_k_g[j])
            oT_l = _flash_attention(bounded, qT_l, k_c, vT_c, k_l, vT_l)
            x = _attn_out(oT_l, woT_bf, x, mod_lat, l1g, l1b)
            if not last:
                oT_c = _flash_attention(bounded, qT_c, k_c, vT_c)
                ctx = _attn_out(oT_c, woT_bf, ctx, mod_ctx, l1g, l1b)
        elif kind == 1:
            xcat = jnp.concatenate([ctx, x], axis=1)
            modcat = jnp.stack([mod_ctx, mod_lat], axis=1)
            w_in = ssd_w_in[j]
            wz = w_in[:, :D_INNER].astype(BF16)
            wx = w_in[:, D_INNER:D_INNER + CONV_DIM].astype(BF16)
            wdt = jnp.pad(w_in[:, D_INNER + CONV_DIM:], ((0, 0), (0, LANES - 2 * SSM_HEADS))).astype(BF16)
            z, xbc, dt_raw = _ssd_in_proj(xcat, modcat, wz, wx, wdt)
            conv_w8 = jnp.pad(ssd_conv_w[j], ((0, 8 - CONV_WIDTH), (0, 0)))
            xs, bm, cm = _ssd_conv(xbc, conv_w8, ssd_conv_b[j][None, :])
            a = -jnp.exp(ssd_a_log[j].astype(F32))
            dtb = ssd_dt_bias[j].astype(F32)
            dsk = jnp.repeat(ssd_d[j], SSM_HEADDIM)[None, :]
            dt_f = dt_raw[:, :, :SSM_HEADS]
            dt_b = dt_raw[:, :, SSM_HEADS:2 * SSM_HEADS]
            y_f = _ssd_scan(xs, bm, cm, dt_f, jnp.swapaxes(dt_f, 1, 2), a[0], dtb[0], False)
            y = _ssd_scan(xs, bm, cm, dt_b, jnp.swapaxes(dt_b, 1, 2), a[1], dtb[1], True, y_f, dsk)
            xcat = _ssd_out(y, z, ssd_norm_g[j][None, :], ssd_w_out[j].astype(BF16), xcat, modcat, l1g, l1b)
            ctx, x = xcat[:, :lc], xcat[:, lc:]
        else:
            pw = pool_w[j].astype(BF16)
            ps = pool_scale[j][None, :]
            x = _pool_mixer(x, mod_lat, pw, ps, l1g, l1b)
            if not last:
                ctx = _pool_mixer(ctx, mod_ctx, pw, ps, l1g, l1b)
        wrT = moe_router[i].T
        if last:
            x, = _moe_layer([(x, mod_lat)], wrT, moe_w_gate, moe_w_up, moe_w_down, i, l2g, l2b)
        else:
            x, ctx = _moe_layer([(x, mod_lat), (ctx, mod_ctx)], wrT, moe_w_gate, moe_w_up, moe_w_down, i, l2g, l2b)
    return x
```
